```python
import math
import jax, jax.numpy as jnp
from jax import lax
import numpy as np


D_MODEL = 2048
BATCH = 2
SEQ = 16384
DEPTH = 2

GRID_W = 64
CTX_LEN = 256
ATTN_WIDTH = D_MODEL // 2
POOL_WIDTH = D_MODEL - ATTN_WIDTH
N_HEADS = 8
HEAD_DIM = ATTN_WIDTH // (2 * N_HEADS)
V_DIM = 2 * HEAD_DIM
POOL_WINDOWS = (2, 4, 8, 16)
N_POOL_GROUPS = len(POOL_WINDOWS)
POOL_GROUP = POOL_WIDTH // N_POOL_GROUPS
IN_WIDTH = 3 * ATTN_WIDTH + POOL_WIDTH
N_EXPERTS = 16
N_EXPERT_GROUPS = 4
EXPERTS_PER_GROUP = N_EXPERTS // N_EXPERT_GROUPS
TOP_K = 2
D_EXPERT = 1408
ROPE_THETA = 10000.0
Q_BLOCK = 128
MOE_BLOCK = 128
EPS = 1e-6

kernel_name = 'hybrid_pool_diffattn_grouped_moe_dit'


def rms_norm(x, g):
    xf = x.astype(jnp.float32)
    y = xf * lax.rsqrt(jnp.mean(xf * xf, axis=-1, keepdims=True) + EPS)
    return (y * g.astype(jnp.float32)).astype(x.dtype)


def modulate(h, shift, scale):
    return h * (1.0 + scale) + shift


def axial_rope_tables(n):
    rows = n // GRID_W
    row = jnp.repeat(jnp.arange(rows), GRID_W).astype(jnp.float32)
    col = jnp.tile(jnp.arange(GRID_W), rows).astype(jnp.float32)
    nf = HEAD_DIM // 4
    inv = ROPE_THETA ** (-jnp.arange(nf, dtype=jnp.float32) / nf)
    ang = jnp.stack([row[:, None] * inv, col[:, None] * inv], axis=1)
    return jnp.cos(ang), jnp.sin(ang)


def apply_axial_rope(x, cos, sin):
    B, n, H, M, Dh = x.shape
    nf = Dh // 4
    xr = x.astype(jnp.float32).reshape(B, n, H, M, 2, 2, nf)
    x1, x2 = xr[..., 0, :], xr[..., 1, :]
    cs = cos[None, :, None, None]
    sn = sin[None, :, None, None]
    out = jnp.stack([x1 * cs - x2 * sn, x2 * cs + x1 * sn], axis=-2)
    return out.reshape(B, n, H, M, Dh).astype(x.dtype)


def diff_softmax_attend(q, k, v, lam):
    s = jnp.einsum('bqhmd,bkhmd->bhmqk', q, k, preferred_element_type=jnp.float32)
    p = jax.nn.softmax(s, axis=-1)
    a = p[:, :, 0] - lam * p[:, :, 1]
    return jnp.einsum('bhqk,bkhe->bqhe', a.astype(v.dtype), v)


def diff_attention_blocks(q, k, v, lam):
    B, n = q.shape[:2]
    nb = n // Q_BLOCK
    qb = q.reshape(B, nb, Q_BLOCK, N_HEADS, 2, HEAD_DIM).transpose(1, 0, 2, 3, 4, 5)
    o = lax.map(lambda qblk: diff_softmax_attend(qblk, k, v, lam), qb)
    return o.transpose(1, 0, 2, 3, 4).reshape(B, n, N_HEADS, V_DIM)


def multi_scale_pool(u, w_pool, pool_scale):
    B, n, _ = u.shape
    ug = u.astype(jnp.float32).reshape(B, n, N_POOL_GROUPS, POOL_GROUP)
    S = jnp.concatenate([jnp.zeros((B, 1, N_POOL_GROUPS, POOL_GROUP), jnp.float32),
                         jnp.cumsum(ug, axis=1)], axis=1)
    t = jnp.arange(n)[:, None]
    half = jnp.array([w // 2 for w in POOL_WINDOWS])[None, :]
    wins = jnp.array(POOL_WINDOWS)[None, :]
    lo = jnp.clip(t - half, 0, n - 1)
    hi = jnp.clip(t - half + wins - 1, 0, n - 1)
    gidx = jnp.arange(N_POOL_GROUPS)[None, :]
    cnt = (hi - lo + 1).astype(jnp.float32)[None, :, :, None]
    mean = (S[:, hi + 1, gidx] - S[:, lo, gidx]) / cnt
    d = (mean - ug).astype(u.dtype)
    y = jnp.einsum('bngc,gcd->bngd', d, w_pool)
    return y.reshape(B, n, POOL_WIDTH) * pool_scale


def route(h, w_router, b_router):
    T = h.shape[0]
    logits = jnp.matmul(h, w_router, preferred_element_type=jnp.float32) + b_router.astype(jnp.float32)
    probs = jax.nn.softmax(logits, axis=-1)
    pg = probs.reshape(T, N_EXPERT_GROUPS, EXPERTS_PER_GROUP)
    group_score = jnp.sum(lax.top_k(pg, 2)[0], axis=-1)
    g = jnp.argmax(group_score, axis=-1)
    in_group = pg[jnp.arange(T), g]
    topv, topi = lax.top_k(in_group, TOP_K)
    experts = g[:, None] * EXPERTS_PER_GROUP + topi
    gates = topv / jnp.sum(topv, axis=-1, keepdims=True)
    return experts, gates


def moe(h, experts, gates, w_gate, w_up, w_down):
    T, D = h.shape
    A = T * TOP_K
    e_flat = experts.reshape(A)
    tok = jnp.repeat(jnp.arange(T), TOP_K)
    g_flat = gates.reshape(A).astype(h.dtype)
    order = jnp.argsort(e_flat)
    e_sorted = e_flat[order]
    counts = jnp.bincount(e_flat, length=N_EXPERTS)
    starts = jnp.cumsum(counts) - counts
    padded = (counts + MOE_BLOCK - 1) // MOE_BLOCK * MOE_BLOCK
    pad_ends = jnp.cumsum(padded)
    pad_starts = pad_ends - padded
    dest = pad_starts[e_sorted] + jnp.arange(A) - starts[e_sorted]
    n_blocks = -(-A // MOE_BLOCK) + N_EXPERTS
    P = n_blocks * MOE_BLOCK
    buf_tok = jnp.full((P,), T, jnp.int32).at[dest].set(tok[order])
    buf_gate = jnp.zeros((P,), h.dtype).at[dest].set(g_flat[order])
    block_e = jnp.minimum(jnp.searchsorted(pad_ends, jnp.arange(n_blocks) * MOE_BLOCK, side='right'),
                          N_EXPERTS - 1)
    h_pad = jnp.concatenate([h, jnp.zeros((1, D), h.dtype)], axis=0)
    xb = h_pad[buf_tok].reshape(n_blocks, MOE_BLOCK, D)

    def expert_block(args):
        xblk, e = args
        a = xblk @ w_gate[e]
        b = xblk @ w_up[e]
        return (jax.nn.silu(a) * b) @ w_down[e]

    yb = lax.map(expert_block, (xb, block_e))
    y = yb.reshape(P, D) * buf_gate[:, None]
    return jnp.zeros((T + 1, D), h.dtype).at[buf_tok].add(y)[:T]


def setup_inputs(seed: int = 0) -> dict:
    key = jax.random.key(seed)
    ks = jax.random.split(key, 24)
    f32 = jnp.float32
    D = D_MODEL

    def nrm(k, shape, s):
        return jax.random.normal(k, shape, f32) * s

    return {
        'x': nrm(ks[0], (BATCH, SEQ, D), 1.0),
        'c': nrm(ks[1], (BATCH, D), 1.0),
        'ctx': nrm(ks[2], (BATCH, CTX_LEN, D), 1.0),
        'c_ctx': nrm(ks[3], (D,), 1.0),
        'w_mod': nrm(ks[4], (DEPTH, D, 6 * D), 0.5 * D ** -0.5),
        'b_mod': nrm(ks[5], (DEPTH, 6 * D), 0.02),
        'norm1': 1.0 + nrm(ks[6], (DEPTH, D), 0.02),
        'norm2': 1.0 + nrm(ks[7], (DEPTH, D), 0.02),
        'w_in': nrm(ks[8], (DEPTH, D, IN_WIDTH), D ** -0.5),
        'q_norm': 1.0 + nrm(ks[9], (DEPTH, HEAD_DIM), 0.02),
        'k_norm': 1.0 + nrm(ks[10], (DEPTH, HEAD_DIM), 0.02),
        'lam_qk': nrm(ks[11], (DEPTH, 4, HEAD_DIM), 0.1),
        'sub_norm': 1.0 + nrm(ks[12], (DEPTH, V_DIM), 0.02),
        'w_pool': nrm(ks[13], (DEPTH, N_POOL_GROUPS, POOL_GROUP, POOL_GROUP), POOL_GROUP ** -0.5),
        'pool_scale': 1.0 + nrm(ks[14], (DEPTH, POOL_WIDTH), 0.02),
        'w_out': nrm(ks[15], (DEPTH, D, D), D ** -0.5),
        'w_router': nrm(ks[16], (D, N_EXPERTS), D ** -0.5),
        'b_router': nrm(ks[17], (N_EXPERTS,), 0.01),
        'w_gate': nrm(ks[18], (DEPTH, N_EXPERTS, D, D_EXPERT), D ** -0.5),
        'w_up': nrm(ks[19], (DEPTH, N_EXPERTS, D, D_EXPERT), D ** -0.5),
        'w_down': nrm(ks[20], (DEPTH, N_EXPERTS, D_EXPERT, D), D_EXPERT ** -0.5),
    }


def reference(x, c, ctx, c_ctx, w_mod, b_mod, norm1, norm2, w_in, q_norm, k_norm, lam_qk, sub_norm,
              w_pool, pool_scale, w_out, w_router, b_router, w_gate, w_up, w_down):
    B, n, D = x.shape
    CL = ctx.shape[1]
    AW = ATTN_WIDTH
    cos, sin = axial_rope_tables(n)
    q_scale = HEAD_DIM ** -0.5
    xc = ctx
    for li in range(DEPTH):
        last = li == DEPTH - 1
        mod = jax.nn.silu(c) @ w_mod[li] + b_mod[li]
        modc = jax.nn.silu(c_ctx) @ w_mod[li] + b_mod[li]
        sh1, sc1, g1, sh2, sc2, g2 = jnp.split(mod[:, None, :], 6, axis=-1)
        sh1c, sc1c, g1c, sh2c, sc2c, g2c = jnp.split(modc, 6)

        lam_init = 0.8 - 0.6 * math.exp(-0.3 * li)
        lv = lam_qk[li].astype(jnp.float32)
        lam = jnp.exp(jnp.sum(lv[0] * lv[1])) - jnp.exp(jnp.sum(lv[2] * lv[3])) + lam_init

        h = modulate(rms_norm(x, norm1[li]), sh1, sc1)
        p = h @ w_in[li]
        q = p[..., :AW].reshape(B, n, N_HEADS, 2, HEAD_DIM)
        k = p[..., AW:2 * AW].reshape(B, n, N_HEADS, 2, HEAD_DIM)
        v = p[..., 2 * AW:3 * AW].reshape(B, n, N_HEADS, V_DIM)
        u = p[..., 3 * AW:]
        q = apply_axial_rope(rms_norm(q, q_norm[li]), cos, sin) * q_scale
        k = apply_axial_rope(rms_norm(k, k_norm[li]), cos, sin)

        hc = modulate(rms_norm(xc, norm1[li]), sh1c, sc1c)
        kvc = hc @ w_in[li][:, AW:3 * AW]
        kc = rms_norm(kvc[..., :AW].reshape(B, CL, N_HEADS, 2, HEAD_DIM), k_norm[li])
        vc = kvc[..., AW:].reshape(B, CL, N_HEADS, V_DIM)

        k_all = jnp.concatenate([k, kc], axis=1)
        v_all = jnp.concatenate([v, vc], axis=1)
        o = diff_attention_blocks(q, k_all, v_all, lam)
        o = rms_norm(o, sub_norm[li]) * (1.0 - lam_init)
        mix = jnp.concatenate([o.reshape(B, n, AW), multi_scale_pool(u, w_pool[li], pool_scale[li])],
                              axis=-1) @ w_out[li]
        x_new = x + g1 * mix

        if not last:
            qc = hc @ w_in[li][:, :AW]
            uc = hc @ w_in[li][:, 3 * AW:]
            qc = rms_norm(qc.reshape(B, CL, N_HEADS, 2, HEAD_DIM), q_norm[li]) * q_scale
            oc = diff_softmax_attend(qc, kc, vc, lam)
            oc = rms_norm(oc, sub_norm[li]) * (1.0 - lam_init)
            mixc = jnp.concatenate([oc.reshape(B, CL, AW), multi_scale_pool(uc, w_pool[li], pool_scale[li])],
                                   axis=-1) @ w_out[li]
            xc = xc + g1c * mixc
        x = x_new

        h2 = modulate(rms_norm(x, norm2[li]), sh2, sc2).reshape(B * n, D)
        if not last:
            h2c = modulate(rms_norm(xc, norm2[li]), sh2c, sc2c).reshape(B * CL, D)
            tokens = jnp.concatenate([h2, h2c], axis=0)
        else:
            tokens = h2
        experts, gates = route(tokens, w_router, b_router)
        y = moe(tokens, experts, gates, w_gate[li], w_up[li], w_down[li])
        x = x + g2 * y[:B * n].reshape(B, n, D)
        if not last:
            xc = xc + g2c * y[B * n:].reshape(B, CL, D)
    return x
```

```python
import functools
import math

import jax
import jax.numpy as jnp
from jax import lax
from jax.experimental import pallas as pl
from jax.experimental.pallas import tpu as pltpu

N_HEADS = 8
HEAD_DIM = 64
V_DIM = 2 * HEAD_DIM
POOL_WINDOWS = (2, 4, 8, 16)
N_EXPERTS = 16
N_EXPERT_GROUPS = 4
EXPERTS_PER_GROUP = N_EXPERTS // N_EXPERT_GROUPS
GRID_W = 64
ROPE_THETA = 10000.0
EPS = 1e-6

V7X_VMEM_BYTES = 64 * 1024 * 1024
SUBLANES = 8
LANES = 128

TOKEN_TILE = 512
Q_TILE = 256
MOE_ROWS = 256
HALO = 8

F32 = jnp.float32
BF16 = jnp.bfloat16
NT_DIMS = (((1,), (1,)), ((), ()))


def _params(semantics, vmem_mb):
    return pltpu.CompilerParams(dimension_semantics=semantics,
                                vmem_limit_bytes=min(vmem_mb * 1024 * 1024, V7X_VMEM_BYTES - (4 << 20)))


def _const_spec(shape):
    nd = len(shape)
    return pl.BlockSpec(shape, lambda *_: (0,) * nd, pipeline_mode=pl.Buffered(1))


def _silu(x):
    return x * jax.nn.sigmoid(x)


def _mod_body(c_ref, w_ref, b_ref, o_ref):
    a = _silu(c_ref[...]).astype(BF16)
    w = w_ref[0].astype(BF16)
    o_ref[0] = jnp.dot(a, w, preferred_element_type=F32) + b_ref[0]


def _mod_call(c8, w_mod, b_mod):
    depth, d, n6 = w_mod.shape
    tn = 1024
    return pl.pallas_call(
        _mod_body,
        out_shape=jax.ShapeDtypeStruct((depth, SUBLANES, n6), F32),
        grid=(depth, n6 // tn),
        in_specs=[pl.BlockSpec((SUBLANES, d), lambda l, j: (0, 0)),
                  pl.BlockSpec((1, d, tn), lambda l, j: (l, 0, j)),
                  pl.BlockSpec((1, 1, tn), lambda l, j: (l, 0, j))],
        out_specs=pl.BlockSpec((1, SUBLANES, tn), lambda l, j: (l, 0, j)),
        compiler_params=_params(("arbitrary", "arbitrary"), 40),
        name="mod",
    )(c8, w_mod, b_mod.reshape(depth, 1, n6))


def _rms_modulate(x, g, sc, sh):
    ms = jnp.mean(x * x, axis=-1, keepdims=True)
    return (x * lax.rsqrt(ms + EPS)) * (g * (1.0 + sc)) + sh


def _inproj_body(x_ref, sh_ref, sc_ref, g_ref, wT_ref, wu_ref, gq_ref, gk_ref, cos_ref, sin_ref,
                 qT_ref, k_ref, vT_ref, u_ref, *, rope, q_scale):
    tm = x_ref.shape[1]
    aw = N_HEADS * V_DIM
    h = _rms_modulate(x_ref[0], g_ref[...], sc_ref[0], sh_ref[0])
    hb = h.astype(BF16)
    pT = lax.dot_general(wT_ref[...], hb, NT_DIMS, preferred_element_type=F32)
    u_ref[0] = jnp.dot(hb, wu_ref[...], preferred_element_type=F32)

    def norm_rope(t, g):
        t3 = t.reshape(2 * N_HEADS, HEAD_DIM, tm)
        ms = jnp.mean(t3 * t3, axis=1, keepdims=True)
        y = (t3 * lax.rsqrt(ms + EPS)) * g[None]
        if rope:
            q4 = HEAD_DIM // 4
            sw = jnp.concatenate([y[:, q4:2 * q4], y[:, 0:q4], y[:, 3 * q4:], y[:, 2 * q4:3 * q4]], axis=1)
            y = y * cos_ref[...][None] + sw * sin_ref[...][None]
        return y.reshape(N_HEADS, V_DIM, tm)

    q = norm_rope(pT[0:aw], gq_ref[...]) * q_scale
    qT_ref[0] = q.astype(BF16)
    k = norm_rope(pT[aw:2 * aw], gk_ref[...])
    for hh in range(N_HEADS):
        k_ref[0, hh] = k[hh].T.astype(BF16)
    vT_ref[0, :, 0] = pT[2 * aw:3 * aw].reshape(N_HEADS, V_DIM, tm).astype(BF16)


def _inproj_call(x, sh, sc, g, wT, wu, gq, gk, cosT, sinT, *, rope):
    b, n, d = x.shape
    tm = min(TOKEN_TILE, n)
    aw = N_HEADS * V_DIM
    pw = wu.shape[1]
    body = functools.partial(_inproj_body, rope=rope, q_scale=HEAD_DIM ** -0.5)
    vec = lambda bb, i: (bb, 0, 0)
    return pl.pallas_call(
        body,
        out_shape=(jax.ShapeDtypeStruct((b, N_HEADS, V_DIM, n), BF16),
                   jax.ShapeDtypeStruct((b, N_HEADS, n, V_DIM), BF16),
                   jax.ShapeDtypeStruct((b, N_HEADS, n // tm, V_DIM, tm), BF16),
                   jax.ShapeDtypeStruct((b, n, pw), F32)),
        grid=(b, n // tm),
        in_specs=[pl.BlockSpec((1, tm, d), lambda bb, i: (bb, i, 0)),
                  pl.BlockSpec((1, 1, d), vec), pl.BlockSpec((1, 1, d), vec),
                  _const_spec((1, d)), _const_spec((3 * aw, d)), _const_spec((d, pw)),
                  _const_spec((HEAD_DIM, 1)), _const_spec((HEAD_DIM, 1)),
                  pl.BlockSpec((HEAD_DIM, tm), lambda bb, i: (0, i)),
                  pl.BlockSpec((HEAD_DIM, tm), lambda bb, i: (0, i))],
        out_specs=(pl.BlockSpec((1, N_HEADS, V_DIM, tm), lambda bb, i: (bb, 0, 0, i)),
                   pl.BlockSpec((1, N_HEADS, tm, V_DIM), lambda bb, i: (bb, 0, i, 0)),
                   pl.BlockSpec((1, N_HEADS, 1, V_DIM, tm), lambda bb, i: (bb, 0, i, 0, 0)),
                   pl.BlockSpec((1, tm, pw), lambda bb, i: (bb, i, 0))),
        compiler_params=_params(("arbitrary", "arbitrary"), 56),
        name="inproj_rope" if rope else "inproj_ctx",
    )(x, sh, sc, g, wT, wu, gq, gk, cosT, sinT)


def _attn_body(*refs, n_key_tiles, key_tile, has_ctx, lam_init):
    if has_ctx:
        (lam_ref, qT_ref, k_ref, vT_ref, kc_ref, vcT_ref, gs_ref,
         o_ref, qbd_ref, m_ref, l_ref, acc_ref) = refs
    else:
        lam_ref, qT_ref, k_ref, vT_ref, gs_ref, o_ref, qbd_ref, m_ref, l_ref, acc_ref = refs
    tq = qT_ref.shape[-1]
    q = qT_ref[0, 0]
    z = jnp.zeros((HEAD_DIM, tq), BF16)
    qbd_ref[0:HEAD_DIM, 0:tq] = q[0:HEAD_DIM]
    qbd_ref[0:HEAD_DIM, tq:] = z
    qbd_ref[HEAD_DIM:, 0:tq] = z
    qbd_ref[HEAD_DIM:, tq:] = q[HEAD_DIM:]
    m_ref[...] = jnp.full(m_ref.shape, -jnp.inf, F32)
    l_ref[...] = jnp.zeros(l_ref.shape, F32)
    acc_ref[...] = jnp.zeros(acc_ref.shape, F32)

    def step(kt, vt):
        s = jnp.dot(kt, qbd_ref[...], preferred_element_type=F32)
        m_prev = m_ref[...]
        m_new = jnp.maximum(m_prev, jnp.max(s, axis=0, keepdims=True))
        alpha = jnp.exp(m_prev - m_new)
        e = jnp.exp(s - m_new)
        l_ref[...] = alpha * l_ref[...] + jnp.sum(e, axis=0, keepdims=True)
        acc_ref[...] = alpha * acc_ref[...] + jnp.dot(vt, e.astype(BF16), preferred_element_type=F32)
        m_ref[...] = m_new

    def loop_body(j, carry):
        start = pl.multiple_of(j * key_tile, key_tile)
        step(k_ref[0, 0, pl.ds(start, key_tile), :], vT_ref[0, 0, j])
        return carry

    lax.fori_loop(0, n_key_tiles, loop_body, 0)
    if has_ctx:
        step(kc_ref[0, 0], vcT_ref[0, 0, 0])

    lv = lam_ref[...]
    lam = (jnp.exp(jnp.sum(lv[0:1] * lv[1:2], axis=1, keepdims=True))
           - jnp.exp(jnp.sum(lv[2:3] * lv[3:4], axis=1, keepdims=True)) + lam_init)
    l = l_ref[...]
    acc = acc_ref[...]
    oT = acc[:, 0:tq] / l[:, 0:tq] - lam * (acc[:, tq:] / l[:, tq:])
    ms = jnp.mean(oT * oT, axis=0, keepdims=True)
    on = (oT * lax.rsqrt(ms + EPS)) * gs_ref[...] * (1.0 - lam_init)
    o_ref[0] = on.T.astype(BF16)


def _attn_call(lam_qk, qT, k, vT, kc, vcT, gs, *, lam_init):
    b, nh, _, n = qT.shape
    nk = k.shape[2]
    n_key_tiles, key_tile = vT.shape[2], vT.shape[4]
    tq = min(Q_TILE, n)
    has_ctx = kc is not None
    body = functools.partial(_attn_body, n_key_tiles=n_key_tiles, key_tile=key_tile,
                             has_ctx=has_ctx, lam_init=lam_init)
    in_specs = [_const_spec((4, HEAD_DIM)),
                pl.BlockSpec((1, 1, V_DIM, tq), lambda bb, hh, i: (bb, hh, 0, i)),
                pl.BlockSpec((1, 1, nk, V_DIM), lambda bb, hh, i: (bb, hh, 0, 0)),
                pl.BlockSpec((1, 1, n_key_tiles, V_DIM, key_tile), lambda bb, hh, i: (bb, hh, 0, 0, 0))]
    args = [lam_qk, qT, k, vT]
    if has_ctx:
        cl = kc.shape[2]
        in_specs += [pl.BlockSpec((1, 1, cl, V_DIM), lambda bb, hh, i: (bb, hh, 0, 0)),
                     pl.BlockSpec((1, 1, 1, V_DIM, cl), lambda bb, hh, i: (bb, hh, 0, 0, 0))]
        args += [kc, vcT]
    in_specs.append(_const_spec((V_DIM, 1)))
    args.append(gs)
    return pl.pallas_call(
        body,
        out_shape=jax.ShapeDtypeStruct((b, n, nh * V_DIM), BF16),
        grid=(b, nh, n // tq),
        in_specs=in_specs,
        out_specs=pl.BlockSpec((1, tq, V_DIM), lambda bb, hh, i: (bb, i, hh)),
        scratch_shapes=[pltpu.VMEM((V_DIM, 2 * tq), BF16),
                        pltpu.VMEM((1, 2 * tq), F32),
                        pltpu.VMEM((1, 2 * tq), F32),
                        pltpu.VMEM((V_DIM, 2 * tq), F32)],
        compiler_params=_params(("arbitrary", "arbitrary", "arbitrary"), 48),
        name="attn_latent" if has_ctx else "attn_ctx",
    )(*args)


def _outproj_body(o_ref, u_ref, up_ref, un_ref, x_ref, g1_ref, wp_ref, ps_ref, wo_ref,
                  xo_ref, ubuf, cat, *, n):
    tm = u_ref.shape[1]
    aw = o_ref.shape[2]
    pg = wp_ref.shape[1]
    i = pl.program_id(1)
    last = pl.num_programs(1) - 1
    ubuf[0:HALO] = jnp.where(i > 0, up_ref[0], 0.0)
    ubuf[HALO:HALO + tm] = u_ref[0]
    ubuf[HALO + tm:] = jnp.where(i < last, un_ref[0], 0.0)
    t = i * tm + lax.broadcasted_iota(jnp.int32, (tm, 1), 0)
    cat[:, 0:aw] = o_ref[0]
    for g, w in enumerate(POOL_WINDOWS):
        half = w // 2
        c0 = g * pg
        acc = ubuf[HALO - half:HALO - half + tm, c0:c0 + pg]
        for s in range(-half + 1, w - half):
            acc = acc + ubuf[HALO + s:HALO + s + tm, c0:c0 + pg]
        lo = jnp.maximum(t - half, 0)
        hi = jnp.minimum(t - half + w - 1, n - 1)
        cnt = (hi - lo + 1).astype(F32)
        dlt = acc / cnt - ubuf[HALO:HALO + tm, c0:c0 + pg]
        y = jnp.dot(dlt.astype(BF16), wp_ref[g], preferred_element_type=F32) * ps_ref[:, c0:c0 + pg]
        cat[:, aw + c0:aw + c0 + pg] = y.astype(BF16)
    mix = jnp.dot(cat[...], wo_ref[...], preferred_element_type=F32)
    xo_ref[0] = x_ref[0] + g1_ref[0] * mix


def _outproj_call(o, u, x, g1, wp, ps, wo):
    b, n, d = x.shape
    tm = min(TOKEN_TILE, n)
    aw, pw = o.shape[2], u.shape[2]
    hb = tm // HALO
    nhalo = n // HALO
    body = functools.partial(_outproj_body, n=n)
    return pl.pallas_call(
        body,
        out_shape=jax.ShapeDtypeStruct((b, n, d), F32),
        grid=(b, n // tm),
        in_specs=[pl.BlockSpec((1, tm, aw), lambda bb, i: (bb, i, 0)),
                  pl.BlockSpec((1, tm, pw), lambda bb, i: (bb, i, 0)),
                  pl.BlockSpec((1, HALO, pw), lambda bb, i: (bb, jnp.maximum(i * hb - 1, 0), 0)),
                  pl.BlockSpec((1, HALO, pw), lambda bb, i: (bb, jnp.minimum((i + 1) * hb, nhalo - 1), 0)),
                  pl.BlockSpec((1, tm, d), lambda bb, i: (bb, i, 0)),
                  pl.BlockSpec((1, 1, d), lambda bb, i: (bb, 0, 0)),
                  _const_spec(wp.shape), _const_spec((1, pw)), _const_spec(wo.shape)],
        out_specs=pl.BlockSpec((1, tm, d), lambda bb, i: (bb, i, 0)),
        scratch_shapes=[pltpu.VMEM((tm + 2 * HALO, pw), F32),
                        pltpu.VMEM((tm, aw + pw), BF16)],
        compiler_params=_params(("arbitrary", "arbitrary"), 48),
        name="outproj",
    )(o, u, u, u, x, g1, wp, ps, wo)


def _first_argmax(vals):
    best = vals[0]
    idx = jnp.zeros(best.shape, jnp.int32)
    for j in range(1, len(vals)):
        c = vals[j] > best
        best = jnp.where(c, vals[j], best)
        idx = jnp.where(c, j, idx)
    return best, idx


def _ln2_body(x_ref, sh_ref, sc_ref, g_ref, wr_ref, br_ref, h_ref, e_ref, gcol_ref):
    tm = x_ref.shape[0]
    h = _rms_modulate(x_ref[...], g_ref[...], sc_ref[0], sh_ref[0])
    h_ref[...] = h
    hh = h.astype(BF16)
    hl = (h - hh.astype(F32)).astype(BF16)
    wr = wr_ref[...]
    wh = wr.astype(BF16)
    wl = (wr - wh.astype(F32)).astype(BF16)
    dg = lambda a, bb: lax.dot_general(a, bb, NT_DIMS, preferred_element_type=F32)
    lg = dg(wh, hh) + dg(wh, hl) + dg(wl, hh) + br_ref[...]
    mx = jnp.max(lg, axis=0, keepdims=True)
    ex = jnp.exp(lg - mx)
    p = ex / jnp.sum(ex, axis=0, keepdims=True)
    rows = [p[e:e + 1] for e in range(N_EXPERTS)]
    scores = []
    for g in range(N_EXPERT_GROUPS):
        r = rows[g * EXPERTS_PER_GROUP:(g + 1) * EXPERTS_PER_GROUP]
        best = None
        for a in range(EXPERTS_PER_GROUP):
            for bb in range(a + 1, EXPERTS_PER_GROUP):
                s = r[a] + r[bb]
                best = s if best is None else jnp.maximum(best, s)
        scores.append(best)
    _, gsel = _first_argmax(scores)
    vals = []
    for j in range(EXPERTS_PER_GROUP):
        v = rows[j]
        for g in range(1, N_EXPERT_GROUPS):
            v = jnp.where(gsel == g, rows[g * EXPERTS_PER_GROUP + j], v)
        vals.append(v)
    v1, i1 = _first_argmax(vals)
    v2 = jnp.full(v1.shape, -jnp.inf, F32)
    i2 = jnp.zeros(i1.shape, jnp.int32)
    for j in range(EXPERTS_PER_GROUP):
        c = jnp.logical_and(i1 != j, vals[j] > v2)
        v2 = jnp.where(c, vals[j], v2)
        i2 = jnp.where(c, j, i2)
    den = v1 + v2
    e1 = gsel * EXPERTS_PER_GROUP + i1
    e2 = gsel * EXPERTS_PER_GROUP + i2
    row2 = lax.broadcasted_iota(jnp.int32, (2, tm), 0)
    e_ref[...] = jnp.where(row2 == 0, e1, e2)
    rowg = lax.broadcasted_iota(jnp.int32, (LANES, tm), 0)
    gates = jnp.where(rowg == 0, v1 / den, jnp.where(rowg == 1, v2 / den, 0.0))
    gcol_ref[...] = gates.T


def _ln2_call(x, sh, sc, g, wr, br):
    b, n, d = x.shape
    tm = min(TOKEN_TILE, n)
    per = n // tm
    t = b * n
    return pl.pallas_call(
        _ln2_body,
        out_shape=(jax.ShapeDtypeStruct((t, d), F32),
                   jax.ShapeDtypeStruct((2, t), jnp.int32),
                   jax.ShapeDtypeStruct((t, LANES), F32)),
        grid=(t // tm,),
        in_specs=[pl.BlockSpec((tm, d), lambda i: (i, 0)),
                  pl.BlockSpec((1, 1, d), lambda i: (i // per, 0, 0)),
                  pl.BlockSpec((1, 1, d), lambda i: (i // per, 0, 0)),
                  _const_spec((1, d)), _const_spec(wr.shape), _const_spec(br.shape)],
        out_specs=(pl.BlockSpec((tm, d), lambda i: (i, 0)),
                   pl.BlockSpec((2, tm), lambda i: (0, i)),
                   pl.BlockSpec((tm, LANES), lambda i: (i, 0))),
        compiler_params=_params(("arbitrary",), 40),
        name="ln2_router",
    )(x.reshape(t, d), sh, sc, g, wr, br)


def _dispatch_body(e_ref, dest_ref, cnt_ref, cnt_sc, carry_sc, start_sc, *, rows):
    ph = pl.program_id(0)
    j = pl.program_id(1)
    tt = e_ref.shape[1]
    e = e_ref[...]
    eid = lax.broadcasted_iota(jnp.int32, (N_EXPERTS, tt), 0)
    oh0 = (eid == e[0:1]).astype(F32)
    oh1 = (eid == e[1:2]).astype(F32)
    tot0 = jnp.sum(oh0, axis=1, keepdims=True)
    tot1 = jnp.sum(oh1, axis=1, keepdims=True)

    @pl.when(jnp.logical_and(ph == 0, j == 0))
    def _():
        cnt_sc[...] = jnp.zeros(cnt_sc.shape, F32)

    @pl.when(ph == 0)
    def _():
        cnt_sc[...] += tot0 + tot1

    @pl.when(jnp.logical_and(ph == 1, j == 0))
    def _():
        cnt = cnt_sc[...]
        padded = jnp.ceil(cnt / rows) * rows
        ecol = lax.broadcasted_iota(jnp.int32, (N_EXPERTS, 1), 0)
        start = jnp.zeros((N_EXPERTS, 1), F32)
        for ee in range(N_EXPERTS - 1):
            start = start + jnp.where(ecol > ee, padded[ee:ee + 1], 0.0)
        start_sc[...] = start
        carry_sc[...] = jnp.zeros(carry_sc.shape, F32)
        cnt_ref[...] = cnt

    @pl.when(ph == 1)
    def _():
        r = lax.broadcasted_iota(jnp.int32, (tt, tt), 0)
        c = lax.broadcasted_iota(jnp.int32, (tt, tt), 1)
        upper = (r < c).astype(BF16)
        pre0 = jnp.dot(oh0.astype(BF16), upper, preferred_element_type=F32)
        pre1 = jnp.dot(oh1.astype(BF16), upper, preferred_element_type=F32)
        base = start_sc[...] + carry_sc[...]
        d0 = jnp.sum(oh0 * (base + pre0), axis=0, keepdims=True)
        d1 = jnp.sum(oh1 * (base + tot0 + pre1), axis=0, keepdims=True)
        row2 = lax.broadcasted_iota(jnp.int32, (2, tt), 0)
        dest_ref[...] = jnp.where(row2 == 0, d0, d1).astype(jnp.int32)
        carry_sc[...] += tot0 + tot1


def _dispatch_call(experts, *, rows):
    t = experts.shape[1]
    tt = TOKEN_TILE
    body = functools.partial(_dispatch_body, rows=rows)
    return pl.pallas_call(
        body,
        out_shape=(jax.ShapeDtypeStruct((2, t), jnp.int32),
                   jax.ShapeDtypeStruct((N_EXPERTS, 1), F32)),
        grid=(2, t // tt),
        in_specs=[pl.BlockSpec((2, tt), lambda ph, j: (0, j))],
        out_specs=(pl.BlockSpec((2, tt), lambda ph, j: (0, j * ph)),
                   pl.BlockSpec((N_EXPERTS, 1), lambda ph, j: (0, 0))),
        scratch_shapes=[pltpu.VMEM((N_EXPERTS, 1), F32)] * 3,
        compiler_params=_params(("arbitrary", "arbitrary"), 32),
        name="dispatch",
    )(experts)


def _row_copy(src, src_row, dst, dst_row, sem):
    return pltpu.make_async_copy(src.at[pl.ds(src_row, 1)], dst.at[pl.ds(dst_row, 1)], sem)


def _scatter_body(dest_ref, h_ref, xs_in_ref, xs_ref, sem, *, n_tokens):
    del xs_in_ref
    tm = h_ref.shape[0]
    base = pl.program_id(0) * tm

    def issue(r, carry):
        for kk in range(2):
            _row_copy(h_ref, r, xs_ref, dest_ref[kk * n_tokens + base + r], sem).start()
        return carry

    lax.fori_loop(0, tm, issue, 0)
    for kk in range(2):
        pltpu.make_async_copy(h_ref, xs_ref.at[pl.ds(0, tm)], sem).wait()


def _scatter_call(dest_flat, tokens, xs_zero):
    t, d = tokens.shape
    tm = TOKEN_TILE
    body = functools.partial(_scatter_body, n_tokens=t)
    return pl.pallas_call(
        body,
        out_shape=jax.ShapeDtypeStruct(xs_zero.shape, xs_zero.dtype),
        grid_spec=pltpu.PrefetchScalarGridSpec(
            num_scalar_prefetch=1,
            grid=(t // tm,),
            in_specs=[pl.BlockSpec((tm, d), lambda i, dst: (i, 0)),
                      pl.BlockSpec(memory_space=pl.ANY)],
            out_specs=pl.BlockSpec(memory_space=pl.ANY),
            scratch_shapes=[pltpu.SemaphoreType.DMA(())]),
        input_output_aliases={2: 0},
        compiler_params=_params(("arbitrary",), 32),
        name="moe_scatter",
    )(dest_flat, tokens, xs_zero)


def _ffn_body(be_ref, nu_ref, xs_ref, wgT_ref, wuT_ref, wdT_ref, ys_ref):
    i = pl.program_id(0)

    @pl.when(i < nu_ref[0])
    def _():
        xb = xs_ref[...].astype(BF16)
        aT = lax.dot_general(wgT_ref[0], xb, NT_DIMS, preferred_element_type=F32)
        bT = lax.dot_general(wuT_ref[0], xb, NT_DIMS, preferred_element_type=F32)
        hT = (_silu(aT) * bT).astype(BF16)
        yT = jnp.dot(wdT_ref[0], hT, preferred_element_type=F32)
        ys_ref[...] = yT.T

    @pl.when(i >= nu_ref[0])
    def _():
        ys_ref[...] = jnp.zeros(ys_ref.shape, F32)


def _ffn_call(block_e, n_used, xs, wgT, wuT, wdT, *, rows):
    p, d = xs.shape
    de = wgT.shape[1]
    return pl.pallas_call(
        _ffn_body,
        out_shape=jax.ShapeDtypeStruct((p, d), F32),
        grid_spec=pltpu.PrefetchScalarGridSpec(
            num_scalar_prefetch=2,
            grid=(p // rows,),
            in_specs=[pl.BlockSpec((rows, d), lambda i, be, nu: (i, 0)),
                      pl.BlockSpec((1, de, d), lambda i, be, nu: (be[i], 0, 0)),
                      pl.BlockSpec((1, de, d), lambda i, be, nu: (be[i], 0, 0)),
                      pl.BlockSpec((1, d, de), lambda i, be, nu: (be[i], 0, 0))],
            out_specs=pl.BlockSpec((rows, d), lambda i, be, nu: (i, 0))),
        compiler_params=_params(("arbitrary",), 56),
        name="moe_ffn",
    )(block_e, n_used, xs, wgT, wuT, wdT)


def _combine_body(dest_ref, x_ref, gcol_ref, g2_ref, ys_ref, xo_ref, ya, yb, sem, *, n_tokens, offset):
    tm = x_ref.shape[0]
    base = offset + pl.program_id(0) * tm

    def issue(r, carry):
        _row_copy(ys_ref, dest_ref[base + r], ya, r, sem).start()
        _row_copy(ys_ref, dest_ref[n_tokens + base + r], yb, r, sem).start()
        return carry

    lax.fori_loop(0, tm, issue, 0)
    pltpu.make_async_copy(ys_ref.at[pl.ds(0, tm)], ya, sem).wait()
    pltpu.make_async_copy(ys_ref.at[pl.ds(0, tm)], yb, sem).wait()
    gc = gcol_ref[...]
    y = gc[:, 0:1] * ya[...] + gc[:, 1:2] * yb[...]
    xo_ref[...] = x_ref[...] + g2_ref[0] * y


def _combine_call(dest_flat, x, gcol, g2, ys, *, n_tokens, offset):
    b, n, d = x.shape
    tm = min(TOKEN_TILE, n)
    per = n // tm
    t = b * n
    off_blocks = offset // tm
    body = functools.partial(_combine_body, n_tokens=n_tokens, offset=offset)
    out = pl.pallas_call(
        body,
        out_shape=jax.ShapeDtypeStruct((t, d), F32),
        grid_spec=pltpu.PrefetchScalarGridSpec(
            num_scalar_prefetch=1,
            grid=(t // tm,),
            in_specs=[pl.BlockSpec((tm, d), lambda i, dst: (i, 0)),
                      pl.BlockSpec((tm, LANES), lambda i, dst: (i + off_blocks, 0)),
                      pl.BlockSpec((1, 1, d), lambda i, dst: (i // per, 0, 0)),
                      pl.BlockSpec(memory_space=pl.ANY)],
            out_specs=pl.BlockSpec((tm, d), lambda i, dst: (i, 0)),
            scratch_shapes=[pltpu.VMEM((tm, d), F32), pltpu.VMEM((tm, d), F32),
                            pltpu.SemaphoreType.DMA(())]),
        compiler_params=_params(("arbitrary",), 40),
        name="moe_combine",
    )(dest_flat, x.reshape(t, d), gcol, g2, ys)
    return out.reshape(b, n, d)


def _rope_tables(n):
    rows = n // GRID_W
    row = jnp.repeat(jnp.arange(rows), GRID_W).astype(F32)
    col = jnp.tile(jnp.arange(GRID_W), rows).astype(F32)
    nf = HEAD_DIM // 4
    inv = ROPE_THETA ** (-jnp.arange(nf, dtype=F32) / nf)
    ar = (row[:, None] * inv).T
    ac = (col[:, None] * inv).T
    cosT = jnp.concatenate([jnp.cos(ar), jnp.cos(ar), jnp.cos(ac), jnp.cos(ac)], axis=0)
    sinT = jnp.concatenate([-jnp.sin(ar), jnp.sin(ar), -jnp.sin(ac), jnp.sin(ac)], axis=0)
    return cosT, sinT


def _moe(tokens, experts, w_gateT, w_upT, w_downT):
    t, d = tokens.shape
    rows = MOE_ROWS
    n_blocks = -(-2 * t // rows) + N_EXPERTS
    dest, counts = _dispatch_call(experts, rows=rows)
    cnt = counts[:, 0].astype(jnp.int32)
    blocks = (cnt + rows - 1) // rows
    ends = jnp.cumsum(blocks)
    n_used = ends[-1]
    bid = jnp.arange(n_blocks, dtype=jnp.int32)
    block_e = jnp.searchsorted(ends, jnp.minimum(bid, n_used - 1), side='right').astype(jnp.int32)
    block_e = jnp.minimum(block_e, N_EXPERTS - 1)
    dest_flat = dest.reshape(2 * t)
    xs = _scatter_call(dest_flat, tokens, jnp.zeros((n_blocks * rows, d), F32))
    ys = _ffn_call(block_e, n_used.reshape(1).astype(jnp.int32), xs, w_gateT, w_upT, w_downT, rows=rows)
    return ys, dest_flat


def kernel(x, c, ctx, c_ctx, w_mod, b_mod, norm1, norm2, w_in, q_norm, k_norm, lam_qk, sub_norm, w_pool,
           pool_scale, w_out, w_router, b_router, w_gate, w_up, w_down):
    b, n, d = x.shape
    cl = ctx.shape[1]
    depth = w_mod.shape[0]
    aw = N_HEADS * V_DIM
    assert n % TOKEN_TILE == 0 and n % GRID_W == 0 and cl % SUBLANES == 0
    assert (b * n) % TOKEN_TILE == 0 and (b * cl) % TOKEN_TILE == 0

    c8 = jnp.concatenate([c, c_ctx[None], jnp.zeros((SUBLANES - b - 1, d), F32)], axis=0)
    mod = _mod_call(c8, w_mod, b_mod)
    cosT, sinT = _rope_tables(n)
    zc = jnp.zeros((HEAD_DIM, cl), F32)
    wrT = w_router.T
    br = b_router.reshape(N_EXPERTS, 1)

    xc = ctx
    for li in range(depth):
        last = li == depth - 1
        lam_init = 0.8 - 0.6 * math.exp(-0.3 * li)
        m6 = mod[li].reshape(SUBLANES, 6, d)
        lat = [m6[0:b, j].reshape(b, 1, d) for j in range(6)]
        cxm = [jnp.broadcast_to(m6[b, j].reshape(1, 1, d), (b, 1, d)) for j in range(6)]

        wi = w_in[li]
        wT = wi[:, :3 * aw].T.astype(BF16)
        wu = wi[:, 3 * aw:].astype(BF16)
        g1n = norm1[li].reshape(1, d)
        g2n = norm2[li].reshape(1, d)
        gq = q_norm[li].reshape(HEAD_DIM, 1)
        gk = k_norm[li].reshape(HEAD_DIM, 1)
        gs = sub_norm[li].reshape(V_DIM, 1)
        wp = w_pool[li].astype(BF16)
        ps = pool_scale[li].reshape(1, -1)
        wo = w_out[li].astype(BF16)

        qT, k, vT, u = _inproj_call(x, lat[0], lat[1], g1n, wT, wu, gq, gk, cosT, sinT, rope=True)
        qcT, kc, vcT, uc = _inproj_call(xc, cxm[0], cxm[1], g1n, wT, wu, gq, gk, zc, zc, rope=False)
        o = _attn_call(lam_qk[li], qT, k, vT, kc, vcT, gs, lam_init=lam_init)
        x = _outproj_call(o, u, x, lat[2], wp, ps, wo)
        if not last:
            oc = _attn_call(lam_qk[li], qcT, kc, vcT, None, None, gs, lam_init=lam_init)
            xc = _outproj_call(oc, uc, xc, cxm[2], wp, ps, wo)

        tok, ex, gcol = _ln2_call(x, lat[3], lat[4], g2n, wrT, br)
        if not last:
            tokc, exc, gcolc = _ln2_call(xc, cxm[3], cxm[4], g2n, wrT, br)
            tok = jnp.concatenate([tok, tokc], axis=0)
            ex = jnp.concatenate([ex, exc], axis=1)
            gcol = jnp.concatenate([gcol, gcolc], axis=0)
        n_tok = tok.shape[0]
        wgT = jnp.swapaxes(w_gate[li], 1, 2).astype(BF16)
        wuT = jnp.swapaxes(w_up[li], 1, 2).astype(BF16)
        wdT = jnp.swapaxes(w_down[li], 1, 2).astype(BF16)
        ys, dest_flat = _moe(tok, ex, wgT, wuT, wdT)
        x = _combine_call(dest_flat, x, gcol, lat[5], ys, n_tokens=n_tok, offset=0)
        if not last:
            xc = _combine_call(dest_flat, xc, gcol, cxm[5], ys, n_tokens=n_tok, offset=b * n)
    return x
```

```python
import functools
import math

import jax
import jax.numpy as jnp
from jax import lax
from jax.experimental import pallas as pl
from jax.experimental.pallas import tpu as pltpu

N_HEADS = 8
HEAD_DIM = 64
V_DIM = 2 * HEAD_DIM
POOL_WINDOWS = (2, 4, 8, 16)
N_EXPERTS = 16
N_EXPERT_GROUPS = 4
EXPERTS_PER_GROUP = N_EXPERTS // N_EXPERT_GROUPS
GRID_W = 64
ROPE_THETA = 10000.0
EPS = 1e-6

V7X_VMEM_BYTES = 64 * 1024 * 1024
SUBLANES = 8
LANES = 128

TOKEN_TILE = 512
Q_TILE = 512
MOE_ROWS = 256
HALO = 8

F32 = jnp.float32
BF16 = jnp.bfloat16
NT_DIMS = (((1,), (1,)), ((), ()))
LOG2_E = math.log2(math.e)


def _params(semantics, vmem_mb):
    return pltpu.CompilerParams(dimension_semantics=semantics,
                                vmem_limit_bytes=min(vmem_mb * 1024 * 1024, V7X_VMEM_BYTES - (4 << 20)))


def _const_spec(shape):
    nd = len(shape)
    return pl.BlockSpec(shape, lambda *_: (0,) * nd, pipeline_mode=pl.Buffered(1))


def _silu(x):
    return x * jax.nn.sigmoid(x)


def _mod_body(c_ref, w_ref, b_ref, o_ref):
    a = _silu(c_ref[...]).astype(BF16)
    w = w_ref[0].astype(BF16)
    o_ref[0] = jnp.dot(a, w, preferred_element_type=F32) + b_ref[0]


def _mod_call(c8, w_mod, b_mod):
    depth, d, n6 = w_mod.shape
    tn = 1024
    return pl.pallas_call(
        _mod_body,
        out_shape=jax.ShapeDtypeStruct((depth, SUBLANES, n6), F32),
        grid=(depth, n6 // tn),
        in_specs=[pl.BlockSpec((SUBLANES, d), lambda l, j: (0, 0)),
                  pl.BlockSpec((1, d, tn), lambda l, j: (l, 0, j)),
                  pl.BlockSpec((1, 1, tn), lambda l, j: (l, 0, j))],
        out_specs=pl.BlockSpec((1, SUBLANES, tn), lambda l, j: (l, 0, j)),
        compiler_params=_params(("arbitrary", "arbitrary"), 40),
        name="mod",
    )(c8, w_mod, b_mod.reshape(depth, 1, n6))


def _rms_modulate(x, g, sc, sh):
    ms = jnp.mean(x * x, axis=-1, keepdims=True)
    return (x * lax.rsqrt(ms + EPS)) * (g * (1.0 + sc)) + sh


def _inproj_body(x_ref, sh_ref, sc_ref, g_ref, wT_ref, wu_ref, gq_ref, gk_ref, cos_ref, sin_ref,
                 qT_ref, k_ref, vT_ref, u_ref, *, rope, q_scale):
    tm = x_ref.shape[1]
    aw = N_HEADS * V_DIM
    h = _rms_modulate(x_ref[0], g_ref[...], sc_ref[0], sh_ref[0])
    hb = h.astype(BF16)
    pT = lax.dot_general(wT_ref[...], hb, NT_DIMS, preferred_element_type=F32)
    u_ref[0] = jnp.dot(hb, wu_ref[...], preferred_element_type=F32)

    def norm_rope(t, g):
        t3 = t.reshape(2 * N_HEADS, HEAD_DIM, tm)
        ms = jnp.mean(t3 * t3, axis=1, keepdims=True)
        y = (t3 * lax.rsqrt(ms + EPS)) * g[None]
        if rope:
            q4 = HEAD_DIM // 4
            sw = jnp.concatenate([y[:, q4:2 * q4], y[:, 0:q4], y[:, 3 * q4:], y[:, 2 * q4:3 * q4]], axis=1)
            y = y * cos_ref[...][None] + sw * sin_ref[...][None]
        return y.reshape(N_HEADS, V_DIM, tm)

    q = norm_rope(pT[0:aw], gq_ref[...]) * q_scale
    qT_ref[0] = q.astype(BF16)
    k = norm_rope(pT[aw:2 * aw], gk_ref[...])
    for hh in range(N_HEADS):
        k_ref[0, hh] = k[hh].T.astype(BF16)
    vT_ref[0, :, 0] = pT[2 * aw:3 * aw].reshape(N_HEADS, V_DIM, tm).astype(BF16)


def _inproj_call(x, sh, sc, g, wT, wu, gq, gk, cosT, sinT, *, rope):
    b, n, d = x.shape
    tm = min(TOKEN_TILE, n)
    aw = N_HEADS * V_DIM
    pw = wu.shape[1]
    body = functools.partial(_inproj_body, rope=rope, q_scale=HEAD_DIM ** -0.5 * LOG2_E)
    vec = lambda bb, i: (bb, 0, 0)
    return pl.pallas_call(
        body,
        out_shape=(jax.ShapeDtypeStruct((b, N_HEADS, V_DIM, n), BF16),
                   jax.ShapeDtypeStruct((b, N_HEADS, n, V_DIM), BF16),
                   jax.ShapeDtypeStruct((b, N_HEADS, n // tm, V_DIM, tm), BF16),
                   jax.ShapeDtypeStruct((b, n, pw), F32)),
        grid=(b, n // tm),
        in_specs=[pl.BlockSpec((1, tm, d), lambda bb, i: (bb, i, 0)),
                  pl.BlockSpec((1, 1, d), vec), pl.BlockSpec((1, 1, d), vec),
                  _const_spec((1, d)), _const_spec((3 * aw, d)), _const_spec((d, pw)),
                  _const_spec((HEAD_DIM, 1)), _const_spec((HEAD_DIM, 1)),
                  pl.BlockSpec((HEAD_DIM, tm), lambda bb, i: (0, i)),
                  pl.BlockSpec((HEAD_DIM, tm), lambda bb, i: (0, i))],
        out_specs=(pl.BlockSpec((1, N_HEADS, V_DIM, tm), lambda bb, i: (bb, 0, 0, i)),
                   pl.BlockSpec((1, N_HEADS, tm, V_DIM), lambda bb, i: (bb, 0, i, 0)),
                   pl.BlockSpec((1, N_HEADS, 1, V_DIM, tm), lambda bb, i: (bb, 0, i, 0, 0)),
                   pl.BlockSpec((1, tm, pw), lambda bb, i: (bb, i, 0))),
        compiler_params=_params(("arbitrary", "arbitrary"), 56),
        name="inproj_rope" if rope else "inproj_ctx",
    )(x, sh, sc, g, wT, wu, gq, gk, cosT, sinT)


def _attn_body(*refs, n_key_tiles, key_tile, has_ctx, lam_init):
    if has_ctx:
        lam_ref, qT_ref, k_ref, vT_ref, kc_ref, vcT_ref, gs_ref, o_ref, qbd_ref, *bufs = refs
    else:
        lam_ref, qT_ref, k_ref, vT_ref, gs_ref, o_ref, qbd_ref, *bufs = refs
    s_bufs, e_bufs, acc_ref = bufs[0:2], bufs[2:4], bufs[4]
    tq = qT_ref.shape[-1]
    q = qT_ref[0, 0]
    z = jnp.zeros((HEAD_DIM, tq), BF16)
    qbd_ref[0:HEAD_DIM, 0:tq] = q[0:HEAD_DIM]
    qbd_ref[0:HEAD_DIM, tq:] = z
    qbd_ref[HEAD_DIM:, 0:tq] = z
    qbd_ref[HEAD_DIM:, tq:] = q[HEAD_DIM:]

    def scores(kt, slot):
        s = jnp.dot(kt, qbd_ref[...], preferred_element_type=F32)
        s_bufs[slot][0:kt.shape[0]] = s
        return jnp.max(s, axis=0, keepdims=True)

    def softmax(rows, slot, mx, m, l):
        m_new = jnp.maximum(m, mx)
        alpha = jnp.exp2(m - m_new)
        e = jnp.exp2(s_bufs[slot][0:rows] - m_new)
        e_bufs[slot][0:rows] = e.astype(BF16)
        return alpha, m_new, alpha * l + jnp.sum(e, axis=0, keepdims=True)

    def pv(rows, slot, alpha, vt):
        acc_ref[...] = alpha * acc_ref[...] + jnp.dot(vt, e_bufs[slot][0:rows], preferred_element_type=F32)

    def k_tile(j):
        return k_ref[0, 0, pl.ds(pl.multiple_of(j * key_tile, key_tile), key_tile), :]

    tk = key_tile
    m = jnp.full((1, 2 * tq), -jnp.inf, F32)
    l = jnp.zeros((1, 2 * tq), F32)
    acc_ref[...] = jnp.zeros(acc_ref.shape, F32)
    if n_key_tiles == 1:
        mx_c = scores(k_tile(0), 1)
    else:
        mx_p = scores(k_tile(0), 0)
        mx_c = scores(k_tile(1), 1)
        a_p, m, l = softmax(tk, 0, mx_p, m, l)

        unroll = max(u for u in (2, 4, 6) if (n_key_tiles - 2) % u == 0)

        def body(t, carry):
            mx_c, a_p, m, l = carry
            for u in range(unroll):
                i = 2 + unroll * t + u
                mx_n = scores(k_tile(i), u % 2)
                a_c, m, l = softmax(tk, 1 - u % 2, mx_c, m, l)
                pv(tk, u % 2, a_p, vT_ref[0, 0, i - 2])
                mx_c, a_p = mx_n, a_c
            return mx_c, a_p, m, l

        mx_c, a_p, m, l = lax.fori_loop(0, (n_key_tiles - 2) // unroll, body, (mx_c, a_p, m, l))
    if has_ctx:
        cl = kc_ref.shape[2]
        mx_x = scores(kc_ref[0, 0], 0)
    a_c, m, l = softmax(tk, 1, mx_c, m, l)
    if n_key_tiles > 1:
        pv(tk, 0, a_p, vT_ref[0, 0, n_key_tiles - 2])
    if has_ctx:
        a_x, m, l = softmax(cl, 0, mx_x, m, l)
    pv(tk, 1, a_c, vT_ref[0, 0, n_key_tiles - 1])
    if has_ctx:
        pv(cl, 0, a_x, vcT_ref[0, 0, 0])
    acc = acc_ref[...]

    lv = lam_ref[...]
    lam = (jnp.exp(jnp.sum(lv[0:1] * lv[1:2], axis=1, keepdims=True))
           - jnp.exp(jnp.sum(lv[2:3] * lv[3:4], axis=1, keepdims=True)) + lam_init)
    oT =acc[:, 0:tq] / l[:, 0:tq] - lam * (acc[:, tq:] / l[:, tq:])
    ms = jnp.mean(oT * oT, axis=0, keepdims=True)
    on = (oT * lax.rsqrt(ms + EPS)) * gs_ref[...] * (1.0 - lam_init)
    o_ref[0] = on.T.astype(BF16)


def _attn_call(lam_qk, qT, k, vT, kc, vcT, gs, *, lam_init):
    b, nh, _, n = qT.shape
    nk = k.shape[2]
    n_key_tiles, key_tile = vT.shape[2], vT.shape[4]
    tq = min(Q_TILE, n)
    has_ctx = kc is not None
    assert n_key_tiles == 1 or n_key_tiles % 2 == 0
    assert not has_ctx or kc.shape[2] <= key_tile
    body = functools.partial(_attn_body, n_key_tiles=n_key_tiles, key_tile=key_tile,
                             has_ctx=has_ctx, lam_init=lam_init)
    in_specs = [_const_spec((4, HEAD_DIM)),
                pl.BlockSpec((1, 1, V_DIM, tq), lambda bb, hh, i: (bb, hh, 0, i)),
                pl.BlockSpec((1, 1, nk, V_DIM), lambda bb, hh, i: (bb, hh, 0, 0)),
                pl.BlockSpec((1, 1, n_key_tiles, V_DIM, key_tile), lambda bb, hh, i: (bb, hh, 0, 0, 0))]
    args = [lam_qk, qT, k, vT]
    if has_ctx:
        cl = kc.shape[2]
        in_specs += [pl.BlockSpec((1, 1, cl, V_DIM), lambda bb, hh, i: (bb, hh, 0, 0)),
                     pl.BlockSpec((1, 1, 1, V_DIM, cl), lambda bb, hh, i: (bb, hh, 0, 0, 0))]
        args += [kc, vcT]
    in_specs.append(_const_spec((V_DIM, 1)))
    args.append(gs)
    return pl.pallas_call(
        body,
        out_shape=jax.ShapeDtypeStruct((b, n, nh * V_DIM), BF16),
        grid=(b, nh, n // tq),
        in_specs=in_specs,
        out_specs=pl.BlockSpec((1, tq, V_DIM), lambda bb, hh, i: (bb, i, hh)),
        scratch_shapes=[pltpu.VMEM((V_DIM, 2 * tq), BF16),
                        pltpu.VMEM((key_tile, 2 * tq), F32), pltpu.VMEM((key_tile, 2 * tq), F32),
                        pltpu.VMEM((key_tile, 2 * tq), BF16), pltpu.VMEM((key_tile, 2 * tq), BF16),
                        pltpu.VMEM((V_DIM, 2 * tq), F32)],
        compiler_params=_params(("arbitrary", "arbitrary", "arbitrary"), 48),
        name="attn_latent" if has_ctx else "attn_ctx",
    )(*args)


def _outproj_body(o_ref, u_ref, up_ref, un_ref, x_ref, g1_ref, wp_ref, ps_ref, wo_ref,
                  xo_ref, ubuf, cat, *, n):
    tm = u_ref.shape[1]
    aw = o_ref.shape[2]
    pg = wp_ref.shape[1]
    i = pl.program_id(1)
    last = pl.num_programs(1) - 1
    ubuf[0:HALO] = jnp.where(i > 0, up_ref[0], 0.0)
    ubuf[HALO:HALO + tm] = u_ref[0]
    ubuf[HALO + tm:] = jnp.where(i < last, un_ref[0], 0.0)
    t = i * tm + lax.broadcasted_iota(jnp.int32, (tm, 1), 0)
    cat[:, 0:aw] = o_ref[0]
    for g, w in enumerate(POOL_WINDOWS):
        half = w // 2
        c0 = g * pg
        acc = ubuf[HALO - half:HALO - half + tm, c0:c0 + pg]
        for s in range(-half + 1, w - half):
            acc = acc + ubuf[HALO + s:HALO + s + tm, c0:c0 + pg]
        lo = jnp.maximum(t - half, 0)
        hi = jnp.minimum(t - half + w - 1, n - 1)
        cnt = (hi - lo + 1).astype(F32)
        dlt = acc / cnt - ubuf[HALO:HALO + tm, c0:c0 + pg]
        y = jnp.dot(dlt.astype(BF16), wp_ref[g], preferred_element_type=F32) * ps_ref[:, c0:c0 + pg]
        cat[:, aw + c0:aw + c0 + pg] = y.astype(BF16)
    mix = jnp.dot(cat[...], wo_ref[...], preferred_element_type=F32)
    xo_ref[0] = x_ref[0] + g1_ref[0] * mix


def _outproj_call(o, u, x, g1, wp, ps, wo):
    b, n, d = x.shape
    tm = min(TOKEN_TILE, n)
    aw, pw = o.shape[2], u.shape[2]
    hb = tm // HALO
    nhalo = n // HALO
    body = functools.partial(_outproj_body, n=n)
    return pl.pallas_call(
        body,
        out_shape=jax.ShapeDtypeStruct((b, n, d), F32),
        grid=(b, n // tm),
        in_specs=[pl.BlockSpec((1, tm, aw), lambda bb, i: (bb, i, 0)),
                  pl.BlockSpec((1, tm, pw), lambda bb, i: (bb, i, 0)),
                  pl.BlockSpec((1, HALO, pw), lambda bb, i: (bb, jnp.maximum(i * hb - 1, 0), 0)),
                  pl.BlockSpec((1, HALO, pw), lambda bb, i: (bb, jnp.minimum((i + 1) * hb, nhalo - 1), 0)),
                  pl.BlockSpec((1, tm, d), lambda bb, i: (bb, i, 0)),
                  pl.BlockSpec((1, 1, d), lambda bb, i: (bb, 0, 0)),
                  _const_spec(wp.shape), _const_spec((1, pw)), _const_spec(wo.shape)],
        out_specs=pl.BlockSpec((1, tm, d), lambda bb, i: (bb, i, 0)),
        scratch_shapes=[pltpu.VMEM((tm + 2 * HALO, pw), F32),
                        pltpu.VMEM((tm, aw + pw), BF16)],
        compiler_params=_params(("arbitrary", "arbitrary"), 48),
        name="outproj",
    )(o, u, u, u, x, g1, wp, ps, wo)


def _first_argmax(vals):
    best = vals[0]
    idx = jnp.zeros(best.shape, jnp.int32)
    for j in range(1, len(vals)):
        c = vals[j] > best
        best = jnp.where(c, vals[j], best)
        idx = jnp.where(c, j, idx)
    return best, idx


def _ln2_body(x_ref, sh_ref, sc_ref, g_ref, wr_ref, br_ref, h_ref, e_ref, gcol_ref):
    tm = x_ref.shape[0]
    h = _rms_modulate(x_ref[...], g_ref[...], sc_ref[0], sh_ref[0])
    h_ref[...] = h
    hh = h.astype(BF16)
    hl = (h - hh.astype(F32)).astype(BF16)
    wr = wr_ref[...]
    wh = wr.astype(BF16)
    wl = (wr - wh.astype(F32)).astype(BF16)
    dg = lambda a, bb: lax.dot_general(a, bb, NT_DIMS, preferred_element_type=F32)
    lg = dg(wh, hh) + dg(wh, hl) + dg(wl, hh) + br_ref[...]
    mx = jnp.max(lg, axis=0, keepdims=True)
    ex = jnp.exp(lg - mx)
    p = ex / jnp.sum(ex, axis=0, keepdims=True)
    rows = [p[e:e + 1] for e in range(N_EXPERTS)]
    scores = []
    for g in range(N_EXPERT_GROUPS):
        r = rows[g * EXPERTS_PER_GROUP:(g + 1) * EXPERTS_PER_GROUP]
        best = None
        for a in range(EXPERTS_PER_GROUP):
            for bb in range(a + 1, EXPERTS_PER_GROUP):
                s = r[a] + r[bb]
                best = s if best is None else jnp.maximum(best, s)
        scores.append(best)
    _, gsel = _first_argmax(scores)
    vals = []
    for j in range(EXPERTS_PER_GROUP):
        v = rows[j]
        for g in range(1, N_EXPERT_GROUPS):
            v = jnp.where(gsel == g, rows[g * EXPERTS_PER_GROUP + j], v)
        vals.append(v)
    v1, i1 = _first_argmax(vals)
    v2 = jnp.full(v1.shape, -jnp.inf, F32)
    i2 = jnp.zeros(i1.shape, jnp.int32)
    for j in range(EXPERTS_PER_GROUP):
        c = jnp.logical_and(i1 != j, vals[j] > v2)
        v2 = jnp.where(c, vals[j], v2)
        i2 = jnp.where(c, j, i2)
    den = v1 + v2
    e1 = gsel * EXPERTS_PER_GROUP + i1
    e2 = gsel * EXPERTS_PER_GROUP + i2
    row2 = lax.broadcasted_iota(jnp.int32, (2, tm), 0)
    e_ref[...] = jnp.where(row2 == 0, e1, e2)
    rowg = lax.broadcasted_iota(jnp.int32, (LANES, tm), 0)
    gates = jnp.where(rowg == 0, v1 / den, jnp.where(rowg == 1, v2 / den, 0.0))
    gcol_ref[...] = gates.T


def _ln2_call(x, sh, sc, g, wr, br):
    b, n, d = x.shape
    tm = min(TOKEN_TILE, n)
    per = n // tm
    t = b * n
    return pl.pallas_call(
        _ln2_body,
        out_shape=(jax.ShapeDtypeStruct((t, d), F32),
                   jax.ShapeDtypeStruct((2, t), jnp.int32),
                   jax.ShapeDtypeStruct((t, LANES), F32)),
        grid=(t // tm,),
        in_specs=[pl.BlockSpec((tm, d), lambda i: (i, 0)),
                  pl.BlockSpec((1, 1, d), lambda i: (i // per, 0, 0)),
                  pl.BlockSpec((1, 1, d), lambda i: (i // per, 0, 0)),
                  _const_spec((1, d)), _const_spec(wr.shape), _const_spec(br.shape)],
        out_specs=(pl.BlockSpec((tm, d), lambda i: (i, 0)),
                   pl.BlockSpec((2, tm), lambda i: (0, i)),
                   pl.BlockSpec((tm, LANES), lambda i: (i, 0))),
        compiler_params=_params(("arbitrary",), 40),
        name="ln2_router",
    )(x.reshape(t, d), sh, sc, g, wr, br)


def _dispatch_body(e_ref, dest_ref, cnt_ref, cnt_sc, carry_sc, start_sc, *, rows):
    ph = pl.program_id(0)
    j = pl.program_id(1)
    tt = e_ref.shape[1]
    e = e_ref[...]
    eid = lax.broadcasted_iota(jnp.int32, (N_EXPERTS, tt), 0)
    oh0 = (eid == e[0:1]).astype(F32)
    oh1 = (eid == e[1:2]).astype(F32)
    tot0 = jnp.sum(oh0, axis=1, keepdims=True)
    tot1 = jnp.sum(oh1, axis=1, keepdims=True)

    @pl.when(jnp.logical_and(ph == 0, j == 0))
    def _():
        cnt_sc[...] = jnp.zeros(cnt_sc.shape, F32)

    @pl.when(ph == 0)
    def _():
        cnt_sc[...] += tot0 + tot1

    @pl.when(jnp.logical_and(ph == 1, j == 0))
    def _():
        cnt = cnt_sc[...]
        padded = jnp.ceil(cnt / rows) * rows
        ecol = lax.broadcasted_iota(jnp.int32, (N_EXPERTS, 1), 0)
        start = jnp.zeros((N_EXPERTS, 1), F32)
        for ee in range(N_EXPERTS - 1):
            start = start + jnp.where(ecol > ee, padded[ee:ee + 1], 0.0)
        start_sc[...] = start
        carry_sc[...] = jnp.zeros(carry_sc.shape, F32)
        cnt_ref[...] = cnt

    @pl.when(ph == 1)
    def _():
        r = lax.broadcasted_iota(jnp.int32, (tt, tt), 0)
        c = lax.broadcasted_iota(jnp.int32, (tt, tt), 1)
        upper = (r < c).astype(BF16)
        pre0 = jnp.dot(oh0.astype(BF16), upper, preferred_element_type=F32)
        pre1 = jnp.dot(oh1.astype(BF16), upper, preferred_element_type=F32)
        base = start_sc[...] + carry_sc[...]
        d0 = jnp.sum(oh0 * (base + pre0), axis=0, keepdims=True)
        d1 = jnp.sum(oh1 * (base + tot0 + pre1), axis=0, keepdims=True)
        row2 = lax.broadcasted_iota(jnp.int32, (2, tt), 0)
        dest_ref[...] = jnp.where(row2 == 0, d0, d1).astype(jnp.int32)
        carry_sc[...] += tot0 + tot1


def _dispatch_call(experts, *, rows):
    t = experts.shape[1]
    tt = TOKEN_TILE
    body = functools.partial(_dispatch_body, rows=rows)
    return pl.pallas_call(
        body,
        out_shape=(jax.ShapeDtypeStruct((2, t), jnp.int32),
                   jax.ShapeDtypeStruct((N_EXPERTS, 1), F32)),
        grid=(2, t // tt),
        in_specs=[pl.BlockSpec((2, tt), lambda ph, j: (0, j))],
        out_specs=(pl.BlockSpec((2, tt), lambda ph, j: (0, j * ph)),
                   pl.BlockSpec((N_EXPERTS, 1), lambda ph, j: (0, 0))),
        scratch_shapes=[pltpu.VMEM((N_EXPERTS, 1), F32)] * 3,
        compiler_params=_params(("arbitrary", "arbitrary"), 32),
        name="dispatch",
    )(experts)


def _row_copy(src, src_row, dst, dst_row, sem):
    return pltpu.make_async_copy(src.at[pl.ds(src_row, 1)], dst.at[pl.ds(dst_row, 1)], sem)


def _scatter_body(dest_ref, h_ref, xs_in_ref, xs_ref, sem, *, n_tokens):
    del xs_in_ref
    tm = h_ref.shape[0]
    base = pl.program_id(0) * tm

    def issue(r, carry):
        for kk in range(2):
            _row_copy(h_ref, r, xs_ref, dest_ref[kk * n_tokens + base + r], sem).start()
        return carry

    lax.fori_loop(0, tm, issue, 0)
    for kk in range(2):
        pltpu.make_async_copy(h_ref, xs_ref.at[pl.ds(0, tm)], sem).wait()


def _scatter_call(dest_flat, tokens, xs_zero):
    t, d = tokens.shape
    tm = TOKEN_TILE
    body = functools.partial(_scatter_body, n_tokens=t)
    return pl.pallas_call(
        body,
        out_shape=jax.ShapeDtypeStruct(xs_zero.shape, xs_zero.dtype),
        grid_spec=pltpu.PrefetchScalarGridSpec(
            num_scalar_prefetch=1,
            grid=(t // tm,),
            in_specs=[pl.BlockSpec((tm, d), lambda i, dst: (i, 0)),
                      pl.BlockSpec(memory_space=pl.ANY)],
            out_specs=pl.BlockSpec(memory_space=pl.ANY),
            scratch_shapes=[pltpu.SemaphoreType.DMA(())]),
        input_output_aliases={2: 0},
        compiler_params=_params(("arbitrary",), 32),
        name="moe_scatter",
    )(dest_flat, tokens, xs_zero)


def _ffn_body(be_ref, nu_ref, xs_ref, wgT_ref, wuT_ref, wdT_ref, ys_ref):
    i = pl.program_id(0)

    @pl.when(i < nu_ref[0])
    def _():
        xb = xs_ref[...].astype(BF16)
        aT = lax.dot_general(wgT_ref[0], xb, NT_DIMS, preferred_element_type=F32)
        bT = lax.dot_general(wuT_ref[0], xb, NT_DIMS, preferred_element_type=F32)
        hT = (_silu(aT) * bT).astype(BF16)
        yT = jnp.dot(wdT_ref[0], hT, preferred_element_type=F32)
        ys_ref[...] = yT.T

    @pl.when(i >= nu_ref[0])
    def _():
        ys_ref[...] = jnp.zeros(ys_ref.shape, F32)


def _ffn_call(block_e, n_used, xs, wgT, wuT, wdT, *, rows):
    p, d = xs.shape
    de = wgT.shape[1]
    return pl.pallas_call(
        _ffn_body,
        out_shape=jax.ShapeDtypeStruct((p, d), F32),
        grid_spec=pltpu.PrefetchScalarGridSpec(
            num_scalar_prefetch=2,
            grid=(p // rows,),
            in_specs=[pl.BlockSpec((rows, d), lambda i, be, nu: (i, 0)),
                      pl.BlockSpec((1, de, d), lambda i, be, nu: (be[i], 0, 0)),
                      pl.BlockSpec((1, de, d), lambda i, be, nu: (be[i], 0, 0)),
                      pl.BlockSpec((1, d, de), lambda i, be, nu: (be[i], 0, 0))],
            out_specs=pl.BlockSpec((rows, d), lambda i, be, nu: (i, 0))),
        compiler_params=_params(("arbitrary",), 56),
        name="moe_ffn",
    )(block_e, n_used, xs, wgT, wuT, wdT)


def _combine_body(dest_ref, x_ref, gcol_ref, g2_ref, ys_ref, xo_ref, ya, yb, sem, *, n_tokens, offset):
    tm = x_ref.shape[0]
    base = offset + pl.program_id(0) * tm

    def issue(r, carry):
        _row_copy(ys_ref, dest_ref[base + r], ya, r, sem).start()
        _row_copy(ys_ref, dest_ref[n_tokens + base + r], yb, r, sem).start()
        return carry

    lax.fori_loop(0, tm, issue, 0)
    pltpu.make_async_copy(ys_ref.at[pl.ds(0, tm)], ya, sem).wait()
    pltpu.make_async_copy(ys_ref.at[pl.ds(0, tm)], yb, sem).wait()
    gc = gcol_ref[...]
    y = gc[:, 0:1] * ya[...] + gc[:, 1:2] * yb[...]
    xo_ref[...] = x_ref[...] + g2_ref[0] * y


def _combine_call(dest_flat, x, gcol, g2, ys, *, n_tokens, offset):
    b, n, d = x.shape
    tm = min(TOKEN_TILE, n)
    per = n // tm
    t = b * n
    off_blocks = offset // tm
    body = functools.partial(_combine_body, n_tokens=n_tokens, offset=offset)
    out = pl.pallas_call(
        body,
        out_shape=jax.ShapeDtypeStruct((t, d), F32),
        grid_spec=pltpu.PrefetchScalarGridSpec(
            num_scalar_prefetch=1,
            grid=(t // tm,),
            in_specs=[pl.BlockSpec((tm, d), lambda i, dst: (i, 0)),
                      pl.BlockSpec((tm, LANES), lambda i, dst: (i + off_blocks, 0)),
                      pl.BlockSpec((1, 1, d), lambda i, dst: (i // per, 0, 0)),
                      pl.BlockSpec(memory_space=pl.ANY)],
            out_specs=pl.BlockSpec((tm, d), lambda i, dst: (i, 0)),
            scratch_shapes=[pltpu.VMEM((tm, d), F32), pltpu.VMEM((tm, d), F32),
                            pltpu.SemaphoreType.DMA(())]),
        compiler_params=_params(("arbitrary",), 40),
        name="moe_combine",
    )(dest_flat, x.reshape(t, d), gcol, g2, ys)
    return out.reshape(b, n, d)


def _rope_tables(n):
    rows = n // GRID_W
    row = jnp.repeat(jnp.arange(rows), GRID_W).astype(F32)
    col = jnp.tile(jnp.arange(GRID_W), rows).astype(F32)
    nf = HEAD_DIM // 4
    inv = ROPE_THETA ** (-jnp.arange(nf, dtype=F32) / nf)
    ar = (row[:, None] * inv).T
    ac = (col[:, None] * inv).T
    cosT = jnp.concatenate([jnp.cos(ar), jnp.cos(ar), jnp.cos(ac), jnp.cos(ac)], axis=0)
    sinT = jnp.concatenate([-jnp.sin(ar), jnp.sin(ar), -jnp.sin(ac), jnp.sin(ac)], axis=0)
    return cosT, sinT


def _moe(tokens, experts, w_gateT, w_upT, w_downT):
    t, d = tokens.shape
    rows = MOE_ROWS
    n_blocks = -(-2 * t // rows) + N_EXPERTS
    dest, counts = _dispatch_call(experts, rows=rows)
    cnt = counts[:, 0].astype(jnp.int32)
    blocks = (cnt + rows - 1) // rows
    ends = jnp.cumsum(blocks)
    n_used = ends[-1]
    bid = jnp.arange(n_blocks, dtype=jnp.int32)
    block_e = jnp.sum(jnp.minimum(bid, n_used - 1)[:, None] >= ends[None, :], axis=1).astype(jnp.int32)
    block_e = jnp.minimum(block_e, N_EXPERTS - 1)
    dest_flat = dest.reshape(2 * t)
    xs = _scatter_call(dest_flat, tokens, jnp.zeros((n_blocks * rows, d), F32))
    ys = _ffn_call(block_e, n_used.reshape(1).astype(jnp.int32), xs, w_gateT, w_upT, w_downT, rows=rows)
    return ys, dest_flat


def kernel(x, c, ctx, c_ctx, w_mod, b_mod, norm1, norm2, w_in, q_norm, k_norm, lam_qk, sub_norm, w_pool,
           pool_scale, w_out, w_router, b_router, w_gate, w_up, w_down):
    b, n, d = x.shape
    cl = ctx.shape[1]
    depth = w_mod.shape[0]
    aw = N_HEADS * V_DIM
    assert n % TOKEN_TILE == 0 and n % GRID_W == 0 and cl % SUBLANES == 0
    assert (b * n) % TOKEN_TILE == 0 and (b * cl) % TOKEN_TILE == 0

    c8 = jnp.concatenate([c, c_ctx[None], jnp.zeros((SUBLANES - b - 1, d), F32)], axis=0)
    mod = _mod_call(c8, w_mod, b_mod)
    cosT, sinT = _rope_tables(n)
    zc = jnp.zeros((HEAD_DIM, cl), F32)
    wrT = w_router.T
    br = b_router.reshape(N_EXPERTS, 1)

    xc = ctx
    for li in range(depth):
        last = li == depth - 1
        lam_init = 0.8 - 0.6 * math.exp(-0.3 * li)
        m6 = mod[li].reshape(SUBLANES, 6, d)
        lat = [m6[0:b, j].reshape(b, 1, d) for j in range(6)]
        cxm = [jnp.broadcast_to(m6[b, j].reshape(1, 1, d), (b, 1, d)) for j in range(6)]

        wi = w_in[li]
        wT = wi[:, :3 * aw].T.astype(BF16)
        wu = wi[:, 3 * aw:].astype(BF16)
        g1n = norm1[li].reshape(1, d)
        g2n = norm2[li].reshape(1, d)
        gq = q_norm[li].reshape(HEAD_DIM, 1)
        gk = k_norm[li].reshape(HEAD_DIM, 1)
        gs = sub_norm[li].reshape(V_DIM, 1)
        wp = w_pool[li].astype(BF16)
        ps = pool_scale[li].reshape(1, -1)
        wo = w_out[li].astype(BF16)

        qT, k, vT, u = _inproj_call(x, lat[0], lat[1], g1n, wT, wu, gq, gk, cosT, sinT, rope=True)
        qcT, kc, vcT, uc = _inproj_call(xc, cxm[0], cxm[1], g1n, wT, wu, gq, gk, zc, zc, rope=False)
        o = _attn_call(lam_qk[li], qT, k, vT, kc, vcT, gs, lam_init=lam_init)
        x = _outproj_call(o, u, x, lat[2], wp, ps, wo)
        if not last:
            oc = _attn_call(lam_qk[li], qcT, kc, vcT, None, None, gs, lam_init=lam_init)
            xc = _outproj_call(oc, uc, xc, cxm[2], wp, ps, wo)

        tok, ex, gcol = _ln2_call(x, lat[3], lat[4], g2n, wrT, br)
        if not last:
            tokc, exc, gcolc = _ln2_call(xc, cxm[3], cxm[4], g2n, wrT, br)
            tok = jnp.concatenate([tok, tokc], axis=0)
            ex = jnp.concatenate([ex, exc], axis=1)
            gcol = jnp.concatenate([gcol, gcolc], axis=0)
        n_tok = tok.shape[0]
        wgT = jnp.swapaxes(w_gate[li], 1, 2).astype(BF16)
        wuT = jnp.swapaxes(w_up[li], 1, 2).astype(BF16)
        wdT = jnp.swapaxes(w_down[li], 1, 2).astype(BF16)
        ys, dest_flat = _moe(tok, ex, wgT, wuT, wdT)
        x = _combine_call(dest_flat, x, gcol, lat[5], ys, n_tokens=n_tok, offset=0)
        if not last:
            xc = _combine_call(dest_flat, xc, gcol, cxm[5], ys, n_tokens=n_tok, offset=b * n)
    return x
```

```python
import functools
import math

import jax
import jax.numpy as jnp
from jax import lax
from jax.experimental import pallas as pl
from jax.experimental.pallas import tpu as pltpu

N_HEADS = 8
HEAD_DIM = 64
V_DIM = 2 * HEAD_DIM
POOL_WINDOWS = (2, 4, 8, 16)
N_EXPERTS = 16
N_EXPERT_GROUPS = 4
EXPERTS_PER_GROUP = N_EXPERTS // N_EXPERT_GROUPS
GRID_W = 64
ROPE_THETA = 10000.0
EPS = 1e-6

V7X_VMEM_BYTES = 64 * 1024 * 1024
SUBLANES = 8
LANES = 128

TOKEN_TILE = 512
Q_TILE = 1024
DMA_ISSUE_UNROLL = 8
MOE_ROWS = 256
HALO = 8

F32 = jnp.float32
BF16 = jnp.bfloat16
NT_DIMS = (((1,), (1,)), ((), ()))
LOG2_E = math.log2(math.e)


def _params(semantics, vmem_mb):
    return pltpu.CompilerParams(dimension_semantics=semantics,
                                vmem_limit_bytes=min(vmem_mb * 1024 * 1024, V7X_VMEM_BYTES - (4 << 20)))


def _const_spec(shape):
    nd = len(shape)
    return pl.BlockSpec(shape, lambda *_: (0,) * nd, pipeline_mode=pl.Buffered(1))


def _silu(x):
    return x * jax.nn.sigmoid(x)


def _mod_body(c_ref, w_ref, b_ref, o_ref):
    a = _silu(c_ref[...]).astype(BF16)
    w = w_ref[0].astype(BF16)
    o_ref[0] = jnp.dot(a, w, preferred_element_type=F32) + b_ref[0]


def _mod_call(c8, w_mod, b_mod):
    depth, d, n6 = w_mod.shape
    tn = 1024
    return pl.pallas_call(
        _mod_body,
        out_shape=jax.ShapeDtypeStruct((depth, SUBLANES, n6), F32),
        grid=(depth, n6 // tn),
        in_specs=[pl.BlockSpec((SUBLANES, d), lambda l, j: (0, 0)),
                  pl.BlockSpec((1, d, tn), lambda l, j: (l, 0, j)),
                  pl.BlockSpec((1, 1, tn), lambda l, j: (l, 0, j))],
        out_specs=pl.BlockSpec((1, SUBLANES, tn), lambda l, j: (l, 0, j)),
        compiler_params=_params(("arbitrary", "arbitrary"), 40),
        name="mod",
    )(c8, w_mod, b_mod.reshape(depth, 1, n6))


def _rms_modulate(x, g, sc, sh):
    ms = jnp.mean(x * x, axis=-1, keepdims=True)
    return (x * lax.rsqrt(ms + EPS)) * (g * (1.0 + sc)) + sh


def _inproj_body(x_ref, sh_ref, sc_ref, g_ref, wT_ref, wu_ref, gq_ref, gk_ref, cos_ref, sin_ref,
                 qT_ref, k_ref, vT_ref, u_ref, *, rope, q_scale):
    tm = x_ref.shape[1]
    aw = N_HEADS * V_DIM
    h = _rms_modulate(x_ref[0], g_ref[...], sc_ref[0], sh_ref[0])
    hb = h.astype(BF16)
    pT = lax.dot_general(wT_ref[...], hb, NT_DIMS, preferred_element_type=F32)
    u_ref[0] = jnp.dot(hb, wu_ref[...], preferred_element_type=F32)

    def norm_rope(t, g):
        t3 = t.reshape(2 * N_HEADS, HEAD_DIM, tm)
        ms = jnp.mean(t3 * t3, axis=1, keepdims=True)
        y = (t3 * lax.rsqrt(ms + EPS)) * g[None]
        if rope:
            q4 = HEAD_DIM // 4
            sw = jnp.concatenate([y[:, q4:2 * q4], y[:, 0:q4], y[:, 3 * q4:], y[:, 2 * q4:3 * q4]], axis=1)
            y = y * cos_ref[...][None] + sw * sin_ref[...][None]
        return y.reshape(N_HEADS, V_DIM, tm)

    q = norm_rope(pT[0:aw], gq_ref[...]) * q_scale
    qT_ref[0] = q.astype(BF16)
    k = norm_rope(pT[aw:2 * aw], gk_ref[...])
    for hh in range(N_HEADS):
        k_ref[0, hh] = k[hh].T.astype(BF16)
    vT_ref[0, :, 0] = pT[2 * aw:3 * aw].reshape(N_HEADS, V_DIM, tm).astype(BF16)


def _inproj_call(x, sh, sc, g, wT, wu, gq, gk, cosT, sinT, *, rope):
    b, n, d = x.shape
    tm = min(TOKEN_TILE, n)
    aw = N_HEADS * V_DIM
    pw = wu.shape[1]
    body = functools.partial(_inproj_body, rope=rope, q_scale=HEAD_DIM ** -0.5 * LOG2_E)
    vec = lambda bb, i: (bb, 0, 0)
    return pl.pallas_call(
        body,
        out_shape=(jax.ShapeDtypeStruct((b, N_HEADS, V_DIM, n), BF16),
                   jax.ShapeDtypeStruct((b, N_HEADS, n, V_DIM), BF16),
                   jax.ShapeDtypeStruct((b, N_HEADS, n // tm, V_DIM, tm), BF16),
                   jax.ShapeDtypeStruct((b, n, pw), F32)),
        grid=(b, n // tm),
        in_specs=[pl.BlockSpec((1, tm, d), lambda bb, i: (bb, i, 0)),
                  pl.BlockSpec((1, 1, d), vec), pl.BlockSpec((1, 1, d), vec),
                  _const_spec((1, d)), _const_spec((3 * aw, d)), _const_spec((d, pw)),
                  _const_spec((HEAD_DIM, 1)), _const_spec((HEAD_DIM, 1)),
                  pl.BlockSpec((HEAD_DIM, tm), lambda bb, i: (0, i)),
                  pl.BlockSpec((HEAD_DIM, tm), lambda bb, i: (0, i))],
        out_specs=(pl.BlockSpec((1, N_HEADS, V_DIM, tm), lambda bb, i: (bb, 0, 0, i)),
                   pl.BlockSpec((1, N_HEADS, tm, V_DIM), lambda bb, i: (bb, 0, i, 0)),
                   pl.BlockSpec((1, N_HEADS, 1, V_DIM, tm), lambda bb, i: (bb, 0, i, 0, 0)),
                   pl.BlockSpec((1, tm, pw), lambda bb, i: (bb, i, 0))),
        compiler_params=_params(("arbitrary", "arbitrary"), 56),
        name="inproj_rope" if rope else "inproj_ctx",
    )(x, sh, sc, g, wT, wu, gq, gk, cosT, sinT)


def _attn_body(*refs, n_key_tiles, key_tile, has_ctx, lam_init):
    if has_ctx:
        lam_ref, qT_ref, k_ref, vT_ref, kc_ref, vcT_ref, gs_ref, o_ref, qbd_ref, *bufs = refs
    else:
        lam_ref, qT_ref, k_ref, vT_ref, gs_ref, o_ref, qbd_ref, *bufs = refs
    s_bufs, e_bufs, acc_ref = bufs[0:2], bufs[2:4], bufs[4]
    tq = qT_ref.shape[-1]
    q = qT_ref[0, 0]
    z = jnp.zeros((HEAD_DIM, tq), BF16)
    qbd_ref[0:HEAD_DIM, 0:tq] = q[0:HEAD_DIM]
    qbd_ref[0:HEAD_DIM, tq:] = z
    qbd_ref[HEAD_DIM:, 0:tq] = z
    qbd_ref[HEAD_DIM:, tq:] = q[HEAD_DIM:]

    def scores(kt, slot):
        s = jnp.dot(kt, qbd_ref[...], preferred_element_type=F32)
        s_bufs[slot][0:kt.shape[0]] = s
        return jnp.max(s, axis=0, keepdims=True)

    def softmax(rows, slot, mx, m, l):
        m_new = jnp.maximum(m, mx)
        alpha = jnp.exp2(m - m_new)
        e = jnp.exp2(s_bufs[slot][0:rows] - m_new)
        e_bufs[slot][0:rows] = e.astype(BF16)
        return alpha, m_new, alpha * l + jnp.sum(e, axis=0, keepdims=True)

    def pv(rows, slot, alpha, vt):
        acc_ref[...] = alpha * acc_ref[...] + jnp.dot(vt, e_bufs[slot][0:rows], preferred_element_type=F32)

    def k_tile(j):
        return k_ref[0, 0, pl.ds(pl.multiple_of(j * key_tile, key_tile), key_tile), :]

    tk = key_tile
    m = jnp.full((1, 2 * tq), -jnp.inf, F32)
    l = jnp.zeros((1, 2 * tq), F32)
    acc_ref[...] = jnp.zeros(acc_ref.shape, F32)
    if n_key_tiles == 1:
        mx_c = scores(k_tile(0), 1)
    else:
        mx_p = scores(k_tile(0), 0)
        mx_c = scores(k_tile(1), 1)
        a_p, m, l = softmax(tk, 0, mx_p, m, l)

        unroll = max(u for u in (2, 4, 6) if (n_key_tiles - 2) % u == 0)

        def body(t, carry):
            mx_c, a_p, m, l = carry
            for u in range(unroll):
                i = 2 + unroll * t + u
                mx_n = scores(k_tile(i), u % 2)
                a_c, m, l = softmax(tk, 1 - u % 2, mx_c, m, l)
                pv(tk, u % 2, a_p, vT_ref[0, 0, i - 2])
                mx_c, a_p = mx_n, a_c
            return mx_c, a_p, m, l

        mx_c, a_p, m, l = lax.fori_loop(0, (n_key_tiles - 2) // unroll, body, (mx_c, a_p, m, l))
    if has_ctx:
        cl = kc_ref.shape[2]
        mx_x = scores(kc_ref[0, 0], 0)
    a_c, m, l = softmax(tk, 1, mx_c, m, l)
    if n_key_tiles > 1:
        pv(tk, 0, a_p, vT_ref[0, 0, n_key_tiles - 2])
    if has_ctx:
        a_x, m, l = softmax(cl, 0, mx_x, m, l)
    pv(tk, 1, a_c, vT_ref[0, 0, n_key_tiles - 1])
    if has_ctx:
        pv(cl, 0, a_x, vcT_ref[0, 0, 0])
    acc = acc_ref[...]

    lv = lam_ref[...]
    lam = (jnp.exp(jnp.sum(lv[0:1] * lv[1:2], axis=1, keepdims=True))
           - jnp.exp(jnp.sum(lv[2:3] * lv[3:4], axis=1, keepdims=True)) + lam_init)
    oT =acc[:, 0:tq] / l[:, 0:tq] - lam * (acc[:, tq:] / l[:, tq:])
    ms = jnp.mean(oT * oT, axis=0, keepdims=True)
    on = (oT * lax.rsqrt(ms + EPS)) * gs_ref[...] * (1.0 - lam_init)
    o_ref[0] = on.T.astype(BF16)


def _attn_call(lam_qk, qT, k, vT, kc, vcT, gs, *, lam_init):
    b, nh, _, n = qT.shape
    nk = k.shape[2]
    n_key_tiles, key_tile = vT.shape[2], vT.shape[4]
    tq = min(Q_TILE, n)
    has_ctx = kc is not None
    assert n_key_tiles == 1 or n_key_tiles % 2 == 0
    assert not has_ctx or kc.shape[2] <= key_tile
    body = functools.partial(_attn_body, n_key_tiles=n_key_tiles, key_tile=key_tile,
                             has_ctx=has_ctx, lam_init=lam_init)
    in_specs = [_const_spec((4, HEAD_DIM)),
                pl.BlockSpec((1, 1, V_DIM, tq), lambda bb, hh, i: (bb, hh, 0, i)),
                pl.BlockSpec((1, 1, nk, V_DIM), lambda bb, hh, i: (bb, hh, 0, 0)),
                pl.BlockSpec((1, 1, n_key_tiles, V_DIM, key_tile), lambda bb, hh, i: (bb, hh, 0, 0, 0))]
    args = [lam_qk, qT, k, vT]
    if has_ctx:
        cl = kc.shape[2]
        in_specs += [pl.BlockSpec((1, 1, cl, V_DIM), lambda bb, hh, i: (bb, hh, 0, 0)),
                     pl.BlockSpec((1, 1, 1, V_DIM, cl), lambda bb, hh, i: (bb, hh, 0, 0, 0))]
        args += [kc, vcT]
    in_specs.append(_const_spec((V_DIM, 1)))
    args.append(gs)
    return pl.pallas_call(
        body,
        out_shape=jax.ShapeDtypeStruct((b, n, nh * V_DIM), BF16),
        grid=(b, nh, n // tq),
        in_specs=in_specs,
        out_specs=pl.BlockSpec((1, tq, V_DIM), lambda bb, hh, i: (bb, i, hh)),
        scratch_shapes=[pltpu.VMEM((V_DIM, 2 * tq), BF16),
                        pltpu.VMEM((key_tile, 2 * tq), F32), pltpu.VMEM((key_tile, 2 * tq), F32),
                        pltpu.VMEM((key_tile, 2 * tq), BF16), pltpu.VMEM((key_tile, 2 * tq), BF16),
                        pltpu.VMEM((V_DIM, 2 * tq), F32)],
        compiler_params=_params(("arbitrary", "arbitrary", "arbitrary"), 48),
        name="attn_latent" if has_ctx else "attn_ctx",
    )(*args)


def _outproj_body(o_ref, u_ref, up_ref, un_ref, x_ref, g1_ref, wp_ref, ps_ref, wo_ref,
                  xo_ref, ubuf, cat, *, n):
    tm = u_ref.shape[1]
    aw = o_ref.shape[2]
    pg = wp_ref.shape[1]
    i = pl.program_id(1)
    last = pl.num_programs(1) - 1
    ubuf[0:HALO] = jnp.where(i > 0, up_ref[0], 0.0)
    ubuf[HALO:HALO + tm] = u_ref[0]
    ubuf[HALO + tm:] = jnp.where(i < last, un_ref[0], 0.0)
    t = i * tm + lax.broadcasted_iota(jnp.int32, (tm, 1), 0)
    cat[:, 0:aw] = o_ref[0]
    for g, w in enumerate(POOL_WINDOWS):
        half = w // 2
        c0 = g * pg
        acc = ubuf[HALO - half:HALO - half + tm, c0:c0 + pg]
        for s in range(-half + 1, w - half):
            acc = acc + ubuf[HALO + s:HALO + s + tm, c0:c0 + pg]
        lo = jnp.maximum(t - half, 0)
        hi = jnp.minimum(t - half + w - 1, n - 1)
        cnt = (hi - lo + 1).astype(F32)
        dlt = acc / cnt - ubuf[HALO:HALO + tm, c0:c0 + pg]
        y = jnp.dot(dlt.astype(BF16), wp_ref[g], preferred_element_type=F32) * ps_ref[:, c0:c0 + pg]
        cat[:, aw + c0:aw + c0 + pg] = y.astype(BF16)
    mix = jnp.dot(cat[...], wo_ref[...], preferred_element_type=F32)
    xo_ref[0] = x_ref[0] + g1_ref[0] * mix


def _outproj_call(o, u, x, g1, wp, ps, wo):
    b, n, d = x.shape
    tm = min(TOKEN_TILE, n)
    aw, pw = o.shape[2], u.shape[2]
    hb = tm // HALO
    nhalo = n // HALO
    body = functools.partial(_outproj_body, n=n)
    return pl.pallas_call(
        body,
        out_shape=jax.ShapeDtypeStruct((b, n, d), F32),
        grid=(b, n // tm),
        in_specs=[pl.BlockSpec((1, tm, aw), lambda bb, i: (bb, i, 0)),
                  pl.BlockSpec((1, tm, pw), lambda bb, i: (bb, i, 0)),
                  pl.BlockSpec((1, HALO, pw), lambda bb, i: (bb, jnp.maximum(i * hb - 1, 0), 0)),
                  pl.BlockSpec((1, HALO, pw), lambda bb, i: (bb, jnp.minimum((i + 1) * hb, nhalo - 1), 0)),
                  pl.BlockSpec((1, tm, d), lambda bb, i: (bb, i, 0)),
                  pl.BlockSpec((1, 1, d), lambda bb, i: (bb, 0, 0)),
                  _const_spec(wp.shape), _const_spec((1, pw)), _const_spec(wo.shape)],
        out_specs=pl.BlockSpec((1, tm, d), lambda bb, i: (bb, i, 0)),
        scratch_shapes=[pltpu.VMEM((tm + 2 * HALO, pw), F32),
                        pltpu.VMEM((tm, aw + pw), BF16)],
        compiler_params=_params(("arbitrary", "arbitrary"), 48),
        name="outproj",
    )(o, u, u, u, x, g1, wp, ps, wo)


def _first_argmax(vals):
    best = vals[0]
    idx = jnp.zeros(best.shape, jnp.int32)
    for j in range(1, len(vals)):
        c = vals[j] > best
        best = jnp.where(c, vals[j], best)
        idx = jnp.where(c, j, idx)
    return best, idx


def _ln2_body(x_ref, sh_ref, sc_ref, g_ref, wr_ref, br_ref, h_ref, e_ref, gcol_ref):
    tm = x_ref.shape[0]
    h = _rms_modulate(x_ref[...], g_ref[...], sc_ref[0], sh_ref[0])
    h_ref[...] = h
    hh = h.astype(BF16)
    hl = (h - hh.astype(F32)).astype(BF16)
    wr = wr_ref[...]
    wh = wr.astype(BF16)
    wl = (wr - wh.astype(F32)).astype(BF16)
    dg = lambda a, bb: lax.dot_general(a, bb, NT_DIMS, preferred_element_type=F32)
    lg = dg(wh, hh) + dg(wh, hl) + dg(wl, hh) + br_ref[...]
    mx = jnp.max(lg, axis=0, keepdims=True)
    ex = jnp.exp(lg - mx)
    p = ex / jnp.sum(ex, axis=0, keepdims=True)
    rows = [p[e:e + 1] for e in range(N_EXPERTS)]
    scores = []
    for g in range(N_EXPERT_GROUPS):
        r = rows[g * EXPERTS_PER_GROUP:(g + 1) * EXPERTS_PER_GROUP]
        best = None
        for a in range(EXPERTS_PER_GROUP):
            for bb in range(a + 1, EXPERTS_PER_GROUP):
                s = r[a] + r[bb]
                best = s if best is None else jnp.maximum(best, s)
        scores.append(best)
    _, gsel = _first_argmax(scores)
    vals = []
    for j in range(EXPERTS_PER_GROUP):
        v = rows[j]
        for g in range(1, N_EXPERT_GROUPS):
            v = jnp.where(gsel == g, rows[g * EXPERTS_PER_GROUP + j], v)
        vals.append(v)
    v1, i1 = _first_argmax(vals)
    v2 = jnp.full(v1.shape, -jnp.inf, F32)
    i2 = jnp.zeros(i1.shape, jnp.int32)
    for j in range(EXPERTS_PER_GROUP):
        c = jnp.logical_and(i1 != j, vals[j] > v2)
        v2 = jnp.where(c, vals[j], v2)
        i2 = jnp.where(c, j, i2)
    den = v1 + v2
    e1 = gsel * EXPERTS_PER_GROUP + i1
    e2 = gsel * EXPERTS_PER_GROUP + i2
    row2 = lax.broadcasted_iota(jnp.int32, (2, tm), 0)
    e_ref[...] = jnp.where(row2 == 0, e1, e2)
    rowg = lax.broadcasted_iota(jnp.int32, (LANES, tm), 0)
    gates = jnp.where(rowg == 0, v1 / den, jnp.where(rowg == 1, v2 / den, 0.0))
    gcol_ref[...] = gates.T


def _ln2_call(x, sh, sc, g, wr, br):
    b, n, d = x.shape
    tm = min(TOKEN_TILE, n)
    per = n // tm
    t = b * n
    return pl.pallas_call(
        _ln2_body,
        out_shape=(jax.ShapeDtypeStruct((t, d), F32),
                   jax.ShapeDtypeStruct((2, t), jnp.int32),
                   jax.ShapeDtypeStruct((t, LANES), F32)),
        grid=(t // tm,),
        in_specs=[pl.BlockSpec((tm, d), lambda i: (i, 0)),
                  pl.BlockSpec((1, 1, d), lambda i: (i // per, 0, 0)),
                  pl.BlockSpec((1, 1, d), lambda i: (i // per, 0, 0)),
                  _const_spec((1, d)), _const_spec(wr.shape), _const_spec(br.shape)],
        out_specs=(pl.BlockSpec((tm, d), lambda i: (i, 0)),
                   pl.BlockSpec((2, tm), lambda i: (0, i)),
                   pl.BlockSpec((tm, LANES), lambda i: (i, 0))),
        compiler_params=_params(("arbitrary",), 40),
        name="ln2_router",
    )(x.reshape(t, d), sh, sc, g, wr, br)


def _dispatch_body(e_ref, dest_ref, cnt_ref, cnt_sc, carry_sc, start_sc, *, rows):
    ph = pl.program_id(0)
    j = pl.program_id(1)
    tt = e_ref.shape[1]
    e = e_ref[...]
    eid = lax.broadcasted_iota(jnp.int32, (N_EXPERTS, tt), 0)
    oh0 = (eid == e[0:1]).astype(F32)
    oh1 = (eid == e[1:2]).astype(F32)
    tot0 = jnp.sum(oh0, axis=1, keepdims=True)
    tot1 = jnp.sum(oh1, axis=1, keepdims=True)

    @pl.when(jnp.logical_and(ph == 0, j == 0))
    def _():
        cnt_sc[...] = jnp.zeros(cnt_sc.shape, F32)

    @pl.when(ph == 0)
    def _():
        cnt_sc[...] += tot0 + tot1

    @pl.when(jnp.logical_and(ph == 1, j == 0))
    def _():
        cnt = cnt_sc[...]
        padded = jnp.ceil(cnt / rows) * rows
        ecol = lax.broadcasted_iota(jnp.int32, (N_EXPERTS, 1), 0)
        start = jnp.zeros((N_EXPERTS, 1), F32)
        for ee in range(N_EXPERTS - 1):
            start = start + jnp.where(ecol > ee, padded[ee:ee + 1], 0.0)
        start_sc[...] = start
        carry_sc[...] = jnp.zeros(carry_sc.shape, F32)
        cnt_ref[...] = cnt

    @pl.when(ph == 1)
    def _():
        r = lax.broadcasted_iota(jnp.int32, (tt, tt), 0)
        c = lax.broadcasted_iota(jnp.int32, (tt, tt), 1)
        upper = (r < c).astype(BF16)
        pre0 = jnp.dot(oh0.astype(BF16), upper, preferred_element_type=F32)
        pre1 = jnp.dot(oh1.astype(BF16), upper, preferred_element_type=F32)
        base = start_sc[...] + carry_sc[...]
        d0 = jnp.sum(oh0 * (base + pre0), axis=0, keepdims=True)
        d1 = jnp.sum(oh1 * (base + tot0 + pre1), axis=0, keepdims=True)
        row2 = lax.broadcasted_iota(jnp.int32, (2, tt), 0)
        dest_ref[...] = jnp.where(row2 == 0, d0, d1).astype(jnp.int32)
        carry_sc[...] += tot0 + tot1


def _dispatch_call(experts, *, rows):
    t = experts.shape[1]
    tt = TOKEN_TILE
    body = functools.partial(_dispatch_body, rows=rows)
    return pl.pallas_call(
        body,
        out_shape=(jax.ShapeDtypeStruct((2, t), jnp.int32),
                   jax.ShapeDtypeStruct((N_EXPERTS, 1), F32)),
        grid=(2, t // tt),
        in_specs=[pl.BlockSpec((2, tt), lambda ph, j: (0, j))],
        out_specs=(pl.BlockSpec((2, tt), lambda ph, j: (0, j * ph)),
                   pl.BlockSpec((N_EXPERTS, 1), lambda ph, j: (0, 0))),
        scratch_shapes=[pltpu.VMEM((N_EXPERTS, 1), F32)] * 3,
        compiler_params=_params(("arbitrary", "arbitrary"), 32),
        name="dispatch",
    )(experts)


def _row_copy(src, src_row, dst, dst_row, sem):
    return pltpu.make_async_copy(src.at[pl.ds(src_row, 1)], dst.at[pl.ds(dst_row, 1)], sem)


def _scatter_body(dest_ref, h_ref, xs_in_ref, xs_ref, sem, *, n_tokens):
    del xs_in_ref
    tm = h_ref.shape[0]
    base = pl.program_id(0) * tm

    def issue(r, carry):
        for kk in range(2):
            _row_copy(h_ref, r, xs_ref, dest_ref[kk * n_tokens + base + r], sem).start()
        return carry

    lax.fori_loop(0, tm, issue, 0, unroll=DMA_ISSUE_UNROLL)
    for kk in range(2):
        pltpu.make_async_copy(h_ref, xs_ref.at[pl.ds(0, tm)], sem).wait()


def _scatter_call(dest_flat, tokens, xs_zero):
    t, d = tokens.shape
    tm = TOKEN_TILE
    body = functools.partial(_scatter_body, n_tokens=t)
    return pl.pallas_call(
        body,
        out_shape=jax.ShapeDtypeStruct(xs_zero.shape, xs_zero.dtype),
        grid_spec=pltpu.PrefetchScalarGridSpec(
            num_scalar_prefetch=1,
            grid=(t // tm,),
            in_specs=[pl.BlockSpec((tm, d), lambda i, dst: (i, 0)),
                      pl.BlockSpec(memory_space=pl.ANY)],
            out_specs=pl.BlockSpec(memory_space=pl.ANY),
            scratch_shapes=[pltpu.SemaphoreType.DMA(())]),
        input_output_aliases={2: 0},
        compiler_params=_params(("arbitrary",), 32),
        name="moe_scatter",
    )(dest_flat, tokens, xs_zero)


def _ffn_body(be_ref, nu_ref, xs_ref, wgT_ref, wuT_ref, wdT_ref, ys_ref):
    i = pl.program_id(0)

    @pl.when(i < nu_ref[0])
    def _():
        xb = xs_ref[...].astype(BF16)
        aT = lax.dot_general(wgT_ref[0], xb, NT_DIMS, preferred_element_type=F32)
        bT = lax.dot_general(wuT_ref[0], xb, NT_DIMS, preferred_element_type=F32)
        hT = (_silu(aT) * bT).astype(BF16)
        yT = jnp.dot(wdT_ref[0], hT, preferred_element_type=F32)
        ys_ref[...] = yT.T

    @pl.when(i >= nu_ref[0])
    def _():
        ys_ref[...] = jnp.zeros(ys_ref.shape, F32)


def _ffn_call(block_e, n_used, xs, wgT, wuT, wdT, *, rows):
    p, d = xs.shape
    de = wgT.shape[1]
    return pl.pallas_call(
        _ffn_body,
        out_shape=jax.ShapeDtypeStruct((p, d), F32),
        grid_spec=pltpu.PrefetchScalarGridSpec(
            num_scalar_prefetch=2,
            grid=(p // rows,),
            in_specs=[pl.BlockSpec((rows, d), lambda i, be, nu: (i, 0)),
                      pl.BlockSpec((1, de, d), lambda i, be, nu: (be[i], 0, 0)),
                      pl.BlockSpec((1, de, d), lambda i, be, nu: (be[i], 0, 0)),
                      pl.BlockSpec((1, d, de), lambda i, be, nu: (be[i], 0, 0))],
            out_specs=pl.BlockSpec((rows, d), lambda i, be, nu: (i, 0))),
        compiler_params=_params(("arbitrary",), 56),
        name="moe_ffn",
    )(block_e, n_used, xs, wgT, wuT, wdT)


def _combine_body(dest_ref, x_ref, gcol_ref, g2_ref, ys_ref, xo_ref, ya, yb, sem, *, n_tokens, offset):
    tm = x_ref.shape[0]
    base = offset + pl.program_id(0) * tm

    def issue(r, carry):
        _row_copy(ys_ref, dest_ref[base + r], ya, r, sem).start()
        _row_copy(ys_ref, dest_ref[n_tokens + base + r], yb, r, sem).start()
        return carry

    lax.fori_loop(0, tm, issue, 0, unroll=DMA_ISSUE_UNROLL)
    pltpu.make_async_copy(ys_ref.at[pl.ds(0, tm)], ya, sem).wait()
    pltpu.make_async_copy(ys_ref.at[pl.ds(0, tm)], yb, sem).wait()
    gc = gcol_ref[...]
    y = gc[:, 0:1] * ya[...] + gc[:, 1:2] * yb[...]
    xo_ref[...] = x_ref[...] + g2_ref[0] * y


def _combine_call(dest_flat, x, gcol, g2, ys, *, n_tokens, offset):
    b, n, d = x.shape
    tm = min(TOKEN_TILE, n)
    per = n // tm
    t = b * n
    off_blocks = offset // tm
    body = functools.partial(_combine_body, n_tokens=n_tokens, offset=offset)
    out = pl.pallas_call(
        body,
        out_shape=jax.ShapeDtypeStruct((t, d), F32),
        grid_spec=pltpu.PrefetchScalarGridSpec(
            num_scalar_prefetch=1,
            grid=(t // tm,),
            in_specs=[pl.BlockSpec((tm, d), lambda i, dst: (i, 0)),
                      pl.BlockSpec((tm, LANES), lambda i, dst: (i + off_blocks, 0)),
                      pl.BlockSpec((1, 1, d), lambda i, dst: (i // per, 0, 0)),
                      pl.BlockSpec(memory_space=pl.ANY)],
            out_specs=pl.BlockSpec((tm, d), lambda i, dst: (i, 0)),
            scratch_shapes=[pltpu.VMEM((tm, d), F32), pltpu.VMEM((tm, d), F32),
                            pltpu.SemaphoreType.DMA(())]),
        compiler_params=_params(("arbitrary",), 40),
        name="moe_combine",
    )(dest_flat, x.reshape(t, d), gcol, g2, ys)
    return out.reshape(b, n, d)


def _rope_tables(n):
    rows = n // GRID_W
    row = jnp.repeat(jnp.arange(rows), GRID_W).astype(F32)
    col = jnp.tile(jnp.arange(GRID_W), rows).astype(F32)
    nf = HEAD_DIM // 4
    inv = ROPE_THETA ** (-jnp.arange(nf, dtype=F32) / nf)
    ar = (row[:, None] * inv).T
    ac = (col[:, None] * inv).T
    cosT = jnp.concatenate([jnp.cos(ar), jnp.cos(ar), jnp.cos(ac), jnp.cos(ac)], axis=0)
    sinT = jnp.concatenate([-jnp.sin(ar), jnp.sin(ar), -jnp.sin(ac), jnp.sin(ac)], axis=0)
    return cosT, sinT


def _moe(tokens, experts, w_gateT, w_upT, w_downT):
    t, d = tokens.shape
    rows = MOE_ROWS
    n_blocks = -(-2 * t // rows) + N_EXPERTS
    dest, counts = _dispatch_call(experts, rows=rows)
    cnt = counts[:, 0].astype(jnp.int32)
    blocks = (cnt + rows - 1) // rows
    ends = jnp.cumsum(blocks)
    n_used = ends[-1]
    bid = jnp.arange(n_blocks, dtype=jnp.int32)
    block_e = jnp.sum(jnp.minimum(bid, n_used - 1)[:, None] >= ends[None, :], axis=1).astype(jnp.int32)
    block_e = jnp.minimum(block_e, N_EXPERTS - 1)
    dest_flat = dest.reshape(2 * t)
    xs = _scatter_call(dest_flat, tokens, jnp.zeros((n_blocks * rows, d), F32))
    ys = _ffn_call(block_e, n_used.reshape(1).astype(jnp.int32), xs, w_gateT, w_upT, w_downT, rows=rows)
    return ys, dest_flat


def kernel(x, c, ctx, c_ctx, w_mod, b_mod, norm1, norm2, w_in, q_norm, k_norm, lam_qk, sub_norm, w_pool,
           pool_scale, w_out, w_router, b_router, w_gate, w_up, w_down):
    b, n, d = x.shape
    cl = ctx.shape[1]
    depth = w_mod.shape[0]
    aw = N_HEADS * V_DIM
    assert n % TOKEN_TILE == 0 and n % GRID_W == 0 and cl % SUBLANES == 0
    assert (b * n) % TOKEN_TILE == 0 and (b * cl) % TOKEN_TILE == 0

    c8 = jnp.concatenate([c, c_ctx[None], jnp.zeros((SUBLANES - b - 1, d), F32)], axis=0)
    mod = _mod_call(c8, w_mod, b_mod)
    cosT, sinT = _rope_tables(n)
    zc = jnp.zeros((HEAD_DIM, cl), F32)
    wrT = w_router.T
    br = b_router.reshape(N_EXPERTS, 1)

    xc = ctx
    for li in range(depth):
        last = li == depth - 1
        lam_init = 0.8 - 0.6 * math.exp(-0.3 * li)
        m6 = mod[li].reshape(SUBLANES, 6, d)
        lat = [m6[0:b, j].reshape(b, 1, d) for j in range(6)]
        cxm = [jnp.broadcast_to(m6[b, j].reshape(1, 1, d), (b, 1, d)) for j in range(6)]

        wi = w_in[li]
        wT = wi[:, :3 * aw].T.astype(BF16)
        wu = wi[:, 3 * aw:].astype(BF16)
        g1n = norm1[li].reshape(1, d)
        g2n = norm2[li].reshape(1, d)
        gq = q_norm[li].reshape(HEAD_DIM, 1)
        gk = k_norm[li].reshape(HEAD_DIM, 1)
        gs = sub_norm[li].reshape(V_DIM, 1)
        wp = w_pool[li].astype(BF16)
        ps = pool_scale[li].reshape(1, -1)
        wo = w_out[li].astype(BF16)

        qT, k, vT, u = _inproj_call(x, lat[0], lat[1], g1n, wT, wu, gq, gk, cosT, sinT, rope=True)
        qcT, kc, vcT, uc = _inproj_call(xc, cxm[0], cxm[1], g1n, wT, wu, gq, gk, zc, zc, rope=False)
        o = _attn_call(lam_qk[li], qT, k, vT, kc, vcT, gs, lam_init=lam_init)
        x = _outproj_call(o, u, x, lat[2], wp, ps, wo)
        if not last:
            oc = _attn_call(lam_qk[li], qcT, kc, vcT, None, None, gs, lam_init=lam_init)
            xc = _outproj_call(oc, uc, xc, cxm[2], wp, ps, wo)

        tok, ex, gcol = _ln2_call(x, lat[3], lat[4], g2n, wrT, br)
        if not last:
            tokc, exc, gcolc = _ln2_call(xc, cxm[3], cxm[4], g2n, wrT, br)
            tok = jnp.concatenate([tok, tokc], axis=0)
            ex = jnp.concatenate([ex, exc], axis=1)
            gcol = jnp.concatenate([gcol, gcolc], axis=0)
        n_tok = tok.shape[0]
        wgT = jnp.swapaxes(w_gate[li], 1, 2).astype(BF16)
        wuT = jnp.swapaxes(w_up[li], 1, 2).astype(BF16)
        wdT = jnp.swapaxes(w_down[li], 1, 2).astype(BF16)
        ys, dest_flat = _moe(tok, ex, wgT, wuT, wdT)
        x = _combine_call(dest_flat, x, gcol, lat[5], ys, n_tokens=n_tok, offset=0)
        if not last:
            xc = _combine_call(dest_flat, xc, gcol, cxm[5], ys, n_tokens=n_tok, offset=b * n)
    return x
```

```python
import functools
import math

import jax
import jax.numpy as jnp
from jax import lax
from jax.experimental import pallas as pl
from jax.experimental.pallas import tpu as pltpu

N_HEADS = 8
HEAD_DIM = 64
V_DIM = 2 * HEAD_DIM
V_ROWS = V_DIM + 16
POOL_WINDOWS = (2, 4, 8, 16)
N_EXPERTS = 16
N_EXPERT_GROUPS = 4
EXPERTS_PER_GROUP = N_EXPERTS // N_EXPERT_GROUPS
GRID_W = 64
ROPE_THETA = 10000.0
EPS = 1e-6

V7X_VMEM_BYTES = 64 * 1024 * 1024
SUBLANES = 8
LANES = 128

TOKEN_TILE = 512
Q_TILE = 1024
DMA_ISSUE_UNROLL = 8
MOE_ROWS = 256
HALO = 8

F32 = jnp.float32
BF16 = jnp.bfloat16
NT_DIMS = (((1,), (1,)), ((), ()))
LOG2_E = math.log2(math.e)


def _params(semantics, vmem_mb):
    return pltpu.CompilerParams(dimension_semantics=semantics,
                                vmem_limit_bytes=min(vmem_mb * 1024 * 1024, V7X_VMEM_BYTES - (4 << 20)))


def _const_spec(shape):
    nd = len(shape)
    return pl.BlockSpec(shape, lambda *_: (0,) * nd, pipeline_mode=pl.Buffered(1))


def _silu(x):
    return x * jax.nn.sigmoid(x)


def _mod_body(c_ref, w_ref, b_ref, o_ref):
    a = _silu(c_ref[...]).astype(BF16)
    w = w_ref[0].astype(BF16)
    o_ref[0] = jnp.dot(a, w, preferred_element_type=F32) + b_ref[0]


def _mod_call(c8, w_mod, b_mod):
    depth, d, n6 = w_mod.shape
    tn = 1024
    return pl.pallas_call(
        _mod_body,
        out_shape=jax.ShapeDtypeStruct((depth, SUBLANES, n6), F32),
        grid=(depth, n6 // tn),
        in_specs=[pl.BlockSpec((SUBLANES, d), lambda l, j: (0, 0)),
                  pl.BlockSpec((1, d, tn), lambda l, j: (l, 0, j)),
                  pl.BlockSpec((1, 1, tn), lambda l, j: (l, 0, j))],
        out_specs=pl.BlockSpec((1, SUBLANES, tn), lambda l, j: (l, 0, j)),
        compiler_params=_params(("arbitrary", "arbitrary"), 40),
        name="mod",
    )(c8, w_mod, b_mod.reshape(depth, 1, n6))


def _rms_modulate(x, g, sc, sh):
    ms = jnp.mean(x * x, axis=-1, keepdims=True)
    return (x * lax.rsqrt(ms + EPS)) * (g * (1.0 + sc)) + sh


def _inproj_body(x_ref, sh_ref, sc_ref, g_ref, wT_ref, wu_ref, gq_ref, gk_ref, cos_ref, sin_ref,
                 qT_ref, k_ref, vT_ref, u_ref, *, rope, q_scale):
    tm = x_ref.shape[1]
    aw = N_HEADS * V_DIM
    h = _rms_modulate(x_ref[0], g_ref[...], sc_ref[0], sh_ref[0])
    hb = h.astype(BF16)
    pT = lax.dot_general(wT_ref[...], hb, NT_DIMS, preferred_element_type=F32)
    u_ref[0] = jnp.dot(hb, wu_ref[...], preferred_element_type=F32)

    def norm_rope(t, g):
        t3 = t.reshape(2 * N_HEADS, HEAD_DIM, tm)
        ms = jnp.mean(t3 * t3, axis=1, keepdims=True)
        y = (t3 * lax.rsqrt(ms + EPS)) * g[None]
        if rope:
            q4 = HEAD_DIM // 4
            sw = jnp.concatenate([y[:, q4:2 * q4], y[:, 0:q4], y[:, 3 * q4:], y[:, 2 * q4:3 * q4]], axis=1)
            y = y * cos_ref[...][None] + sw * sin_ref[...][None]
        return y.reshape(N_HEADS, V_DIM, tm)

    q = norm_rope(pT[0:aw], gq_ref[...]) * q_scale
    qT_ref[0] = q.astype(BF16)
    k = norm_rope(pT[aw:2 * aw], gk_ref[...])
    for hh in range(N_HEADS):
        k_ref[0, hh] = k[hh].T.astype(BF16)
    vT_ref[0, :, 0, 0:V_DIM] = pT[2 * aw:3 * aw].reshape(N_HEADS, V_DIM, tm).astype(BF16)
    vT_ref[0, :, 0, V_DIM:] = jnp.ones((N_HEADS, V_ROWS - V_DIM, tm), BF16)


def _inproj_call(x, sh, sc, g, wT, wu, gq, gk, cosT, sinT, *, rope):
    b, n, d = x.shape
    tm = min(TOKEN_TILE, n)
    aw = N_HEADS * V_DIM
    pw = wu.shape[1]
    body = functools.partial(_inproj_body, rope=rope, q_scale=HEAD_DIM ** -0.5 * LOG2_E)
    vec = lambda bb, i: (bb, 0, 0)
    return pl.pallas_call(
        body,
        out_shape=(jax.ShapeDtypeStruct((b, N_HEADS, V_DIM, n), BF16),
                   jax.ShapeDtypeStruct((b, N_HEADS, n, V_DIM), BF16),
                   jax.ShapeDtypeStruct((b, N_HEADS, n // tm, V_ROWS, tm), BF16),
                   jax.ShapeDtypeStruct((b, n, pw), F32)),
        grid=(b, n // tm),
        in_specs=[pl.BlockSpec((1, tm, d), lambda bb, i: (bb, i, 0)),
                  pl.BlockSpec((1, 1, d), vec), pl.BlockSpec((1, 1, d), vec),
                  _const_spec((1, d)), _const_spec((3 * aw, d)), _const_spec((d, pw)),
                  _const_spec((HEAD_DIM, 1)), _const_spec((HEAD_DIM, 1)),
                  pl.BlockSpec((HEAD_DIM, tm), lambda bb, i: (0, i)),
                  pl.BlockSpec((HEAD_DIM, tm), lambda bb, i: (0, i))],
        out_specs=(pl.BlockSpec((1, N_HEADS, V_DIM, tm), lambda bb, i: (bb, 0, 0, i)),
                   pl.BlockSpec((1, N_HEADS, tm, V_DIM), lambda bb, i: (bb, 0, i, 0)),
                   pl.BlockSpec((1, N_HEADS, 1, V_ROWS, tm), lambda bb, i: (bb, 0, i, 0, 0)),
                   pl.BlockSpec((1, tm, pw), lambda bb, i: (bb, i, 0))),
        compiler_params=_params(("arbitrary", "arbitrary"), 56),
        name="inproj_rope" if rope else "inproj_ctx",
    )(x, sh, sc, g, wT, wu, gq, gk, cosT, sinT)


def _attn_body(*refs, n_key_tiles, key_tile, has_ctx, lam_init):
    if has_ctx:
        lam_ref, qT_ref, k_ref, vT_ref, kc_ref, vcT_ref, gs_ref, o_ref, qbd_ref, *bufs = refs
    else:
        lam_ref, qT_ref, k_ref, vT_ref, gs_ref, o_ref, qbd_ref, *bufs = refs
    s_bufs, e_bufs, acc_ref = bufs[0:2], bufs[2:4], bufs[4]
    tq = qT_ref.shape[-1]
    q = qT_ref[0, 0]
    z = jnp.zeros((HEAD_DIM, tq), BF16)
    qbd_ref[0:HEAD_DIM, 0:tq] = q[0:HEAD_DIM]
    qbd_ref[0:HEAD_DIM, tq:] = z
    qbd_ref[HEAD_DIM:, 0:tq] = z
    qbd_ref[HEAD_DIM:, tq:] = q[HEAD_DIM:]

    def scores(kt, slot):
        s = jnp.dot(kt, qbd_ref[...], preferred_element_type=F32)
        s_bufs[slot][0:kt.shape[0]] = s
        return jnp.max(s, axis=0, keepdims=True)

    def softmax(rows, slot, mx, m):
        m_new = jnp.maximum(m, mx)
        alpha = jnp.exp2(m - m_new)
        e_bufs[slot][0:rows] = jnp.exp2((s_bufs[slot][0:rows] - m_new).astype(BF16))
        return alpha, m_new

    def pv(rows, slot, alpha, vt):
        acc_ref[...] = alpha * acc_ref[...] + jnp.dot(vt, e_bufs[slot][0:rows], preferred_element_type=F32)

    def k_tile(j):
        return k_ref[0, 0, pl.ds(pl.multiple_of(j * key_tile, key_tile), key_tile), :]

    tk = key_tile
    m = jnp.full((1, 2 * tq), -jnp.inf, F32)
    acc_ref[...] = jnp.zeros(acc_ref.shape, F32)
    if n_key_tiles == 1:
        mx_c = scores(k_tile(0), 1)
    else:
        mx_p = scores(k_tile(0), 0)
        mx_c = scores(k_tile(1), 1)
        a_p, m = softmax(tk, 0, mx_p, m)

        unroll = max(u for u in (2, 4, 6) if (n_key_tiles - 2) % u == 0)

        def body(t, carry):
            mx_c, a_p, m = carry
            for u in range(unroll):
                i = 2 + unroll * t + u
                mx_n = scores(k_tile(i), u % 2)
                a_c, m = softmax(tk, 1 - u % 2, mx_c, m)
                pv(tk, u % 2, a_p, vT_ref[0, 0, i - 2])
                mx_c, a_p = mx_n, a_c
            return mx_c, a_p, m

        mx_c, a_p, m = lax.fori_loop(0, (n_key_tiles - 2) // unroll, body, (mx_c, a_p, m))
    if has_ctx:
        cl = kc_ref.shape[2]
        mx_x = scores(kc_ref[0, 0], 0)
    a_c, m = softmax(tk, 1, mx_c, m)
    if n_key_tiles > 1:
        pv(tk, 0, a_p, vT_ref[0, 0, n_key_tiles - 2])
    if has_ctx:
        a_x, m = softmax(cl, 0, mx_x, m)
    pv(tk, 1, a_c, vT_ref[0, 0, n_key_tiles - 1])
    if has_ctx:
        pv(cl, 0, a_x, vcT_ref[0, 0, 0])
    acc = acc_ref[0:V_DIM]
    l = acc_ref[V_DIM:V_DIM + 1]

    lv = lam_ref[...]
    lam = (jnp.exp(jnp.sum(lv[0:1] * lv[1:2], axis=1, keepdims=True))
           - jnp.exp(jnp.sum(lv[2:3] * lv[3:4], axis=1, keepdims=True)) + lam_init)
    oT =acc[:, 0:tq] / l[:, 0:tq] - lam * (acc[:, tq:] / l[:, tq:])
    ms = jnp.mean(oT * oT, axis=0, keepdims=True)
    on = (oT * lax.rsqrt(ms + EPS)) * gs_ref[...] * (1.0 - lam_init)
    o_ref[0] = on.T.astype(BF16)


def _attn_call(lam_qk, qT, k, vT, kc, vcT, gs, *, lam_init):
    b, nh, _, n = qT.shape
    nk = k.shape[2]
    n_key_tiles, key_tile = vT.shape[2], vT.shape[4]
    tq = min(Q_TILE, n)
    has_ctx = kc is not None
    assert n_key_tiles == 1 or n_key_tiles % 2 == 0
    assert not has_ctx or kc.shape[2] <= key_tile
    body = functools.partial(_attn_body, n_key_tiles=n_key_tiles, key_tile=key_tile,
                             has_ctx=has_ctx, lam_init=lam_init)
    in_specs = [_const_spec((4, HEAD_DIM)),
                pl.BlockSpec((1, 1, V_DIM, tq), lambda bb, hh, i: (bb, hh, 0, i)),
                pl.BlockSpec((1, 1, nk, V_DIM), lambda bb, hh, i: (bb, hh, 0, 0)),
                pl.BlockSpec((1, 1, n_key_tiles, V_ROWS, key_tile), lambda bb, hh, i: (bb, hh, 0, 0, 0))]
    args = [lam_qk, qT, k, vT]
    if has_ctx:
        cl = kc.shape[2]
        in_specs += [pl.BlockSpec((1, 1, cl, V_DIM), lambda bb, hh, i: (bb, hh, 0, 0)),
                     pl.BlockSpec((1, 1, 1, V_ROWS, cl), lambda bb, hh, i: (bb, hh, 0, 0, 0))]
        args += [kc, vcT]
    in_specs.append(_const_spec((V_DIM, 1)))
    args.append(gs)
    return pl.pallas_call(
        body,
        out_shape=jax.ShapeDtypeStruct((b, n, nh * V_DIM), BF16),
        grid=(b, nh, n // tq),
        in_specs=in_specs,
        out_specs=pl.BlockSpec((1, tq, V_DIM), lambda bb, hh, i: (bb, i, hh)),
        scratch_shapes=[pltpu.VMEM((V_DIM, 2 * tq), BF16),
                        pltpu.VMEM((key_tile, 2 * tq), F32), pltpu.VMEM((key_tile, 2 * tq), F32),
                        pltpu.VMEM((key_tile, 2 * tq), BF16), pltpu.VMEM((key_tile, 2 * tq), BF16),
                        pltpu.VMEM((V_ROWS, 2 * tq), F32)],
        compiler_params=_params(("arbitrary", "arbitrary", "arbitrary"), 48),
        name="attn_latent" if has_ctx else "attn_ctx",
    )(*args)


def _outproj_body(o_ref, u_ref, up_ref, un_ref, x_ref, g1_ref, wp_ref, ps_ref, wo_ref,
                  xo_ref, ubuf, cat, *, n):
    tm = u_ref.shape[1]
    aw = o_ref.shape[2]
    pg = wp_ref.shape[1]
    i = pl.program_id(1)
    last = pl.num_programs(1) - 1
    ubuf[0:HALO] = jnp.where(i > 0, up_ref[0], 0.0)
    ubuf[HALO:HALO + tm] = u_ref[0]
    ubuf[HALO + tm:] = jnp.where(i < last, un_ref[0], 0.0)
    t = i * tm + lax.broadcasted_iota(jnp.int32, (tm, 1), 0)
    cat[:, 0:aw] = o_ref[0]
    for g, w in enumerate(POOL_WINDOWS):
        half = w // 2
        c0 = g * pg
        acc = ubuf[HALO - half:HALO - half + tm, c0:c0 + pg]
        for s in range(-half + 1, w - half):
            acc = acc + ubuf[HALO + s:HALO + s + tm, c0:c0 + pg]
        lo = jnp.maximum(t - half, 0)
        hi = jnp.minimum(t - half + w - 1, n - 1)
        cnt = (hi - lo + 1).astype(F32)
        dlt = acc / cnt - ubuf[HALO:HALO + tm, c0:c0 + pg]
        y = jnp.dot(dlt.astype(BF16), wp_ref[g], preferred_element_type=F32) * ps_ref[:, c0:c0 + pg]
        cat[:, aw + c0:aw + c0 + pg] = y.astype(BF16)
    mix = jnp.dot(cat[...], wo_ref[...], preferred_element_type=F32)
    xo_ref[0] = x_ref[0] + g1_ref[0] * mix


def _outproj_call(o, u, x, g1, wp, ps, wo):
    b, n, d = x.shape
    tm = min(TOKEN_TILE, n)
    aw, pw = o.shape[2], u.shape[2]
    hb = tm // HALO
    nhalo = n // HALO
    body = functools.partial(_outproj_body, n=n)
    return pl.pallas_call(
        body,
        out_shape=jax.ShapeDtypeStruct((b, n, d), F32),
        grid=(b, n // tm),
        in_specs=[pl.BlockSpec((1, tm, aw), lambda bb, i: (bb, i, 0)),
                  pl.BlockSpec((1, tm, pw), lambda bb, i: (bb, i, 0)),
                  pl.BlockSpec((1, HALO, pw), lambda bb, i: (bb, jnp.maximum(i * hb - 1, 0), 0)),
                  pl.BlockSpec((1, HALO, pw), lambda bb, i: (bb, jnp.minimum((i + 1) * hb, nhalo - 1), 0)),
                  pl.BlockSpec((1, tm, d), lambda bb, i: (bb, i, 0)),
                  pl.BlockSpec((1, 1, d), lambda bb, i: (bb, 0, 0)),
                  _const_spec(wp.shape), _const_spec((1, pw)), _const_spec(wo.shape)],
        out_specs=pl.BlockSpec((1, tm, d), lambda bb, i: (bb, i, 0)),
        scratch_shapes=[pltpu.VMEM((tm + 2 * HALO, pw), F32),
                        pltpu.VMEM((tm, aw + pw), BF16)],
        compiler_params=_params(("arbitrary", "arbitrary"), 48),
        name="outproj",
    )(o, u, u, u, x, g1, wp, ps, wo)


def _first_argmax(vals):
    best = vals[0]
    idx = jnp.zeros(best.shape, jnp.int32)
    for j in range(1, len(vals)):
        c = vals[j] > best
        best = jnp.where(c, vals[j], best)
        idx = jnp.where(c, j, idx)
    return best, idx


def _ln2_body(x_ref, sh_ref, sc_ref, g_ref, wr_ref, br_ref, *rest):
    h_ref, e_ref, gcol_ref = rest[-3:]
    tm = x_ref.shape[0]
    h = _rms_modulate(x_ref[...], g_ref[...], sc_ref[0], sh_ref[0])
    h_ref[...] = h
    hh = h.astype(BF16)
    hl = (h - hh.astype(F32)).astype(BF16)
    wr = wr_ref[...]
    wh = wr.astype(BF16)
    wl = (wr - wh.astype(F32)).astype(BF16)
    dg = lambda a, bb: lax.dot_general(a, bb, NT_DIMS, preferred_element_type=F32)
    lg = dg(wh, hh) + dg(wh, hl) + dg(wl, hh) + br_ref[...]
    mx = jnp.max(lg, axis=0, keepdims=True)
    ex = jnp.exp(lg - mx)
    p = ex / jnp.sum(ex, axis=0, keepdims=True)
    rows = [p[e:e + 1] for e in range(N_EXPERTS)]
    scores = []
    for g in range(N_EXPERT_GROUPS):
        r = rows[g * EXPERTS_PER_GROUP:(g + 1) * EXPERTS_PER_GROUP]
        best = None
        for a in range(EXPERTS_PER_GROUP):
            for bb in range(a + 1, EXPERTS_PER_GROUP):
                s = r[a] + r[bb]
                best = s if best is None else jnp.maximum(best, s)
        scores.append(best)
    _, gsel = _first_argmax(scores)
    vals = []
    for j in range(EXPERTS_PER_GROUP):
        v = rows[j]
        for g in range(1, N_EXPERT_GROUPS):
            v = jnp.where(gsel == g, rows[g * EXPERTS_PER_GROUP + j], v)
        vals.append(v)
    v1, i1 = _first_argmax(vals)
    v2 = jnp.full(v1.shape, -jnp.inf, F32)
    i2 = jnp.zeros(i1.shape, jnp.int32)
    for j in range(EXPERTS_PER_GROUP):
        c = jnp.logical_and(i1 != j, vals[j] > v2)
        v2 = jnp.where(c, vals[j], v2)
        i2 = jnp.where(c, j, i2)
    den = v1 + v2
    e1 = gsel * EXPERTS_PER_GROUP + i1
    e2 = gsel * EXPERTS_PER_GROUP + i2
    row2 = lax.broadcasted_iota(jnp.int32, (2, tm), 0)
    e_ref[...] = jnp.where(row2 == 0, e1, e2)
    rowg = lax.broadcasted_iota(jnp.int32, (LANES, tm), 0)
    gates = jnp.where(rowg == 0, v1 / den, jnp.where(rowg == 1, v2 / den, 0.0))
    gcol_ref[...] = gates.T


def _ln2_call(x, sh, sc, g, wr, br, *, total, offset=0, bufs=None):
    b, n, d = x.shape
    tm = min(TOKEN_TILE, n)
    per = n // tm
    t = b * n
    off = offset // tm
    in_specs = [pl.BlockSpec((tm, d), lambda i: (i, 0)),
                pl.BlockSpec((1, 1, d), lambda i: (i // per, 0, 0)),
                pl.BlockSpec((1, 1, d), lambda i: (i // per, 0, 0)),
                _const_spec((1, d)), _const_spec(wr.shape), _const_spec(br.shape)]
    args = [x.reshape(t, d), sh, sc, g, wr, br]
    aliases = {}
    if bufs is not None:
        in_specs += [pl.BlockSpec(memory_space=pl.ANY)] * 3
        aliases = {len(args) + j: j for j in range(3)}
        args += list(bufs)
    return pl.pallas_call(
        _ln2_body,
        out_shape=(jax.ShapeDtypeStruct((total, d), F32),
                   jax.ShapeDtypeStruct((2, total), jnp.int32),
                   jax.ShapeDtypeStruct((total, LANES), F32)),
        grid=(t // tm,),
        in_specs=in_specs,
        out_specs=(pl.BlockSpec((tm, d), lambda i: (i + off, 0)),
                   pl.BlockSpec((2, tm), lambda i: (0, i + off)),
                   pl.BlockSpec((tm, LANES), lambda i: (i + off, 0))),
        input_output_aliases=aliases,
        compiler_params=_params(("arbitrary",), 40),
        name="ln2_router",
    )(*args)


def _dispatch_body(e_ref, dest_ref, cnt_ref, cnt_sc, carry_sc, start_sc, *, rows):
    ph = pl.program_id(0)
    j = pl.program_id(1)
    tt = e_ref.shape[1]
    e = e_ref[...]
    eid = lax.broadcasted_iota(jnp.int32, (N_EXPERTS, tt), 0)
    oh0 = (eid == e[0:1]).astype(F32)
    oh1 = (eid == e[1:2]).astype(F32)
    tot0 = jnp.sum(oh0, axis=1, keepdims=True)
    tot1 = jnp.sum(oh1, axis=1, keepdims=True)

    @pl.when(jnp.logical_and(ph == 0, j == 0))
    def _():
        cnt_sc[...] = jnp.zeros(cnt_sc.shape, F32)

    @pl.when(ph == 0)
    def _():
        cnt_sc[...] += tot0 + tot1

    @pl.when(jnp.logical_and(ph == 1, j == 0))
    def _():
        cnt = cnt_sc[...]
        padded = jnp.ceil(cnt / rows) * rows
        ecol = lax.broadcasted_iota(jnp.int32, (N_EXPERTS, 1), 0)
        start = jnp.zeros((N_EXPERTS, 1), F32)
        for ee in range(N_EXPERTS - 1):
            start = start + jnp.where(ecol > ee, padded[ee:ee + 1], 0.0)
        start_sc[...] = start
        carry_sc[...] = jnp.zeros(carry_sc.shape, F32)
        cnt_ref[...] = cnt

    @pl.when(ph == 1)
    def _():
        r = lax.broadcasted_iota(jnp.int32, (tt, tt), 0)
        c = lax.broadcasted_iota(jnp.int32, (tt, tt), 1)
        upper = (r < c).astype(BF16)
        pre0 = jnp.dot(oh0.astype(BF16), upper, preferred_element_type=F32)
        pre1 = jnp.dot(oh1.astype(BF16), upper, preferred_element_type=F32)
        base = start_sc[...] + carry_sc[...]
        d0 = jnp.sum(oh0 * (base + pre0), axis=0, keepdims=True)
        d1 = jnp.sum(oh1 * (base + tot0 + pre1), axis=0, keepdims=True)
        row2 = lax.broadcasted_iota(jnp.int32, (2, tt), 0)
        dest_ref[...] = jnp.where(row2 == 0, d0, d1).astype(jnp.int32)
        carry_sc[...] += tot0 + tot1


def _dispatch_call(experts, *, rows):
    t = experts.shape[1]
    tt = TOKEN_TILE
    body = functools.partial(_dispatch_body, rows=rows)
    return pl.pallas_call(
        body,
        out_shape=(jax.ShapeDtypeStruct((2, t), jnp.int32),
                   jax.ShapeDtypeStruct((N_EXPERTS, 1), F32)),
        grid=(2, t // tt),
        in_specs=[pl.BlockSpec((2, tt), lambda ph, j: (0, j))],
        out_specs=(pl.BlockSpec((2, tt), lambda ph, j: (0, j * ph)),
                   pl.BlockSpec((N_EXPERTS, 1), lambda ph, j: (0, 0))),
        scratch_shapes=[pltpu.VMEM((N_EXPERTS, 1), F32)] * 3,
        compiler_params=_params(("arbitrary", "arbitrary"), 32),
        name="dispatch",
    )(experts)


def _row_copy(src, src_row, dst, dst_row, sem):
    return pltpu.make_async_copy(src.at[pl.ds(src_row, 1)], dst.at[pl.ds(dst_row, 1)], sem)


def _scatter_body(dest_ref, h_ref, xs_in_ref, xs_ref, sem, *, n_tokens):
    del xs_in_ref
    tm = h_ref.shape[0]
    base = pl.program_id(0) * tm

    def issue(r, carry):
        for kk in range(2):
            _row_copy(h_ref, r, xs_ref, dest_ref[kk * n_tokens + base + r], sem).start()
        return carry

    lax.fori_loop(0, tm, issue, 0, unroll=DMA_ISSUE_UNROLL)
    for kk in range(2):
        pltpu.make_async_copy(h_ref, xs_ref.at[pl.ds(0, tm)], sem).wait()


def _scatter_call(dest_flat, tokens, xs_zero):
    t, d = tokens.shape
    tm = TOKEN_TILE
    body = functools.partial(_scatter_body, n_tokens=t)
    return pl.pallas_call(
        body,
        out_shape=jax.ShapeDtypeStruct(xs_zero.shape, xs_zero.dtype),
        grid_spec=pltpu.PrefetchScalarGridSpec(
            num_scalar_prefetch=1,
            grid=(t // tm,),
            in_specs=[pl.BlockSpec((tm, d), lambda i, dst: (i, 0)),
                      pl.BlockSpec(memory_space=pl.ANY)],
            out_specs=pl.BlockSpec(memory_space=pl.ANY),
            scratch_shapes=[pltpu.SemaphoreType.DMA(())]),
        input_output_aliases={2: 0},
        compiler_params=_params(("arbitrary",), 32),
        name="moe_scatter",
    )(dest_flat, tokens, xs_zero)


def _ffn_body(be_ref, nu_ref, xs_ref, wgT_ref, wuT_ref, wdT_ref, ys_ref):
    i = pl.program_id(0)

    @pl.when(i < nu_ref[0])
    def _():
        xb = xs_ref[...].astype(BF16)
        aT = lax.dot_general(wgT_ref[0, 0], xb, NT_DIMS, preferred_element_type=F32)
        bT = lax.dot_general(wuT_ref[0, 0], xb, NT_DIMS, preferred_element_type=F32)
        hT = (_silu(aT) * bT).astype(BF16)
        yT = jnp.dot(wdT_ref[0, 0], hT, preferred_element_type=F32)
        ys_ref[...] = yT.T

    @pl.when(i >= nu_ref[0])
    def _():
        ys_ref[...] = jnp.zeros(ys_ref.shape, F32)


def _ffn_call(block_e, n_used, xs, wgT, wuT, wdT, *, rows, layer):
    p, d = xs.shape
    de = wgT.shape[2]
    return pl.pallas_call(
        _ffn_body,
        out_shape=jax.ShapeDtypeStruct((p, d), F32),
        grid_spec=pltpu.PrefetchScalarGridSpec(
            num_scalar_prefetch=2,
            grid=(p // rows,),
            in_specs=[pl.BlockSpec((rows, d), lambda i, be, nu: (i, 0)),
                      pl.BlockSpec((1, 1, de, d), lambda i, be, nu: (layer, be[i], 0, 0)),
                      pl.BlockSpec((1, 1, de, d), lambda i, be, nu: (layer, be[i], 0, 0)),
                      pl.BlockSpec((1, 1, d, de), lambda i, be, nu: (layer, be[i], 0, 0))],
            out_specs=pl.BlockSpec((rows, d), lambda i, be, nu: (i, 0))),
        compiler_params=_params(("arbitrary",), 56),
        name="moe_ffn",
    )(block_e, n_used, xs, wgT, wuT, wdT)


def _combine_body(dest_ref, x_ref, gcol_ref, g2_ref, ys_ref, xo_ref, ya, yb, sem, *, n_tokens, offset):
    tm = x_ref.shape[0]
    base = offset + pl.program_id(0) * tm

    def issue(r, carry):
        _row_copy(ys_ref, dest_ref[base + r], ya, r, sem).start()
        _row_copy(ys_ref, dest_ref[n_tokens + base + r], yb, r, sem).start()
        return carry

    lax.fori_loop(0, tm, issue, 0, unroll=DMA_ISSUE_UNROLL)
    pltpu.make_async_copy(ys_ref.at[pl.ds(0, tm)], ya, sem).wait()
    pltpu.make_async_copy(ys_ref.at[pl.ds(0, tm)], yb, sem).wait()
    gc = gcol_ref[...]
    y = gc[:, 0:1] * ya[...] + gc[:, 1:2] * yb[...]
    xo_ref[...] = x_ref[...] + g2_ref[0] * y


def _combine_call(dest_flat, x, gcol, g2, ys, *, n_tokens, offset):
    b, n, d = x.shape
    tm = min(TOKEN_TILE, n)
    per = n // tm
    t = b * n
    off_blocks = offset // tm
    body = functools.partial(_combine_body, n_tokens=n_tokens, offset=offset)
    out = pl.pallas_call(
        body,
        out_shape=jax.ShapeDtypeStruct((t, d), F32),
        grid_spec=pltpu.PrefetchScalarGridSpec(
            num_scalar_prefetch=1,
            grid=(t // tm,),
            in_specs=[pl.BlockSpec((tm, d), lambda i, dst: (i, 0)),
                      pl.BlockSpec((tm, LANES), lambda i, dst: (i + off_blocks, 0)),
                      pl.BlockSpec((1, 1, d), lambda i, dst: (i // per, 0, 0)),
                      pl.BlockSpec(memory_space=pl.ANY)],
            out_specs=pl.BlockSpec((tm, d), lambda i, dst: (i, 0)),
            scratch_shapes=[pltpu.VMEM((tm, d), F32), pltpu.VMEM((tm, d), F32),
                            pltpu.SemaphoreType.DMA(())]),
        compiler_params=_params(("arbitrary",), 40),
        name="moe_combine",
    )(dest_flat, x.reshape(t, d), gcol, g2, ys)
    return out.reshape(b, n, d)


def _rope_tables(n):
    rows = n // GRID_W
    row = jnp.repeat(jnp.arange(rows), GRID_W).astype(F32)
    col = jnp.tile(jnp.arange(GRID_W), rows).astype(F32)
    nf = HEAD_DIM // 4
    inv = ROPE_THETA ** (-jnp.arange(nf, dtype=F32) / nf)
    ar = (row[:, None] * inv).T
    ac = (col[:, None] * inv).T
    cosT = jnp.concatenate([jnp.cos(ar), jnp.cos(ar), jnp.cos(ac), jnp.cos(ac)], axis=0)
    sinT = jnp.concatenate([-jnp.sin(ar), jnp.sin(ar), -jnp.sin(ac), jnp.sin(ac)], axis=0)
    return cosT, sinT


def _moe(tokens, experts, w_gateT, w_upT, w_downT, layer):
    t, d = tokens.shape
    rows = MOE_ROWS
    n_blocks = -(-2 * t // rows) + N_EXPERTS
    dest, counts = _dispatch_call(experts, rows=rows)
    cnt = counts[:, 0].astype(jnp.int32)
    blocks = (cnt + rows - 1) // rows
    ends = jnp.cumsum(blocks)
    n_used = ends[-1]
    bid = jnp.arange(n_blocks, dtype=jnp.int32)
    block_e = jnp.sum(jnp.minimum(bid, n_used - 1)[:, None] >= ends[None, :], axis=1).astype(jnp.int32)
    block_e = jnp.minimum(block_e, N_EXPERTS - 1)
    dest_flat = dest.reshape(2 * t)
    xs = _scatter_call(dest_flat, tokens, jnp.zeros((n_blocks * rows, d), F32))
    ys = _ffn_call(block_e, n_used.reshape(1).astype(jnp.int32), xs, w_gateT, w_upT, w_downT,
                   rows=rows, layer=layer)
    return ys, dest_flat


def kernel(x, c, ctx, c_ctx, w_mod, b_mod, norm1, norm2, w_in, q_norm, k_norm, lam_qk, sub_norm, w_pool,
           pool_scale, w_out, w_router, b_router, w_gate, w_up, w_down):
    b, n, d = x.shape
    cl = ctx.shape[1]
    depth = w_mod.shape[0]
    aw = N_HEADS * V_DIM
    assert n % TOKEN_TILE == 0 and n % GRID_W == 0 and cl % SUBLANES == 0
    assert (b * n) % TOKEN_TILE == 0 and (b * cl) % TOKEN_TILE == 0

    c8 = jnp.concatenate([c, c_ctx[None], jnp.zeros((SUBLANES - b - 1, d), F32)], axis=0)
    mod = _mod_call(c8, w_mod, b_mod)
    cosT, sinT = _rope_tables(n)
    zc = jnp.zeros((HEAD_DIM, cl), F32)
    wrT = w_router.T
    br = b_router.reshape(N_EXPERTS, 1)
    wgT = jnp.swapaxes(w_gate, 2, 3).astype(BF16)
    wuT = jnp.swapaxes(w_up, 2, 3).astype(BF16)
    wdT = jnp.swapaxes(w_down, 2, 3).astype(BF16)

    xc = ctx
    for li in range(depth):
        last = li == depth - 1
        lam_init = 0.8 - 0.6 * math.exp(-0.3 * li)
        m6 = mod[li].reshape(SUBLANES, 6, d)
        lat = [m6[0:b, j].reshape(b, 1, d) for j in range(6)]
        cxm = [jnp.broadcast_to(m6[b, j].reshape(1, 1, d), (b, 1, d)) for j in range(6)]

        wi = w_in[li]
        wT = wi[:, :3 * aw].T.astype(BF16)
        wu = wi[:, 3 * aw:].astype(BF16)
        g1n = norm1[li].reshape(1, d)
        g2n = norm2[li].reshape(1, d)
        gq = q_norm[li].reshape(HEAD_DIM, 1)
        gk = k_norm[li].reshape(HEAD_DIM, 1)
        gs = sub_norm[li].reshape(V_DIM, 1)
        wp = w_pool[li].astype(BF16)
        ps = pool_scale[li].reshape(1, -1)
        wo = w_out[li].astype(BF16)

        qT, k, vT, u = _inproj_call(x, lat[0], lat[1], g1n, wT, wu, gq, gk, cosT, sinT, rope=True)
        qcT, kc, vcT, uc = _inproj_call(xc, cxm[0], cxm[1], g1n, wT, wu, gq, gk, zc, zc, rope=False)
        o = _attn_call(lam_qk[li], qT, k, vT, kc, vcT, gs, lam_init=lam_init)
        x = _outproj_call(o, u, x, lat[2], wp, ps, wo)
        if not last:
            oc = _attn_call(lam_qk[li], qcT, kc, vcT, None, None, gs, lam_init=lam_init)
            xc = _outproj_call(oc, uc, xc, cxm[2], wp, ps, wo)

        n_tok = b * n if last else b * (n + cl)
        tok, ex, gcol = _ln2_call(x, lat[3], lat[4], g2n, wrT, br, total=n_tok)
        if not last:
            tok, ex, gcol = _ln2_call(xc, cxm[3], cxm[4], g2n, wrT, br, total=n_tok, offset=b * n,
                                      bufs=(tok, ex, gcol))
        ys, dest_flat = _moe(tok, ex, wgT, wuT, wdT, li)
        x = _combine_call(dest_flat, x, gcol, lat[5], ys, n_tokens=n_tok, offset=0)
        if not last:
            xc = _combine_call(dest_flat, xc, gcol, cxm[5], ys, n_tokens=n_tok, offset=b * n)
    return x
```

```python
import functools
import math

import jax
import jax.numpy as jnp
from jax import lax
from jax.experimental import pallas as pl
from jax.experimental.pallas import tpu as pltpu

N_HEADS = 8
HEAD_DIM = 64
V_DIM = 2 * HEAD_DIM
V_ROWS = V_DIM + 16
POOL_WINDOWS = (2, 4, 8, 16)
N_EXPERTS = 16
N_EXPERT_GROUPS = 4
EXPERTS_PER_GROUP = N_EXPERTS // N_EXPERT_GROUPS
GRID_W = 64
ROPE_THETA = 10000.0
EPS = 1e-6

V7X_VMEM_BYTES = 64 * 1024 * 1024
SUBLANES = 8
LANES = 128

TOKEN_TILE = 512
Q_TILE = 1024
DMA_ISSUE_UNROLL = 8
MOE_ROWS = 256
HALO = 8

F32 = jnp.float32
BF16 = jnp.bfloat16
NT_DIMS = (((1,), (1,)), ((), ()))
LOG2_E = math.log2(math.e)


def _params(semantics, vmem_mb):
    return pltpu.CompilerParams(dimension_semantics=semantics,
                                vmem_limit_bytes=min(vmem_mb * 1024 * 1024, V7X_VMEM_BYTES - (4 << 20)))


def _const_spec(shape):
    nd = len(shape)
    return pl.BlockSpec(shape, lambda *_: (0,) * nd, pipeline_mode=pl.Buffered(1))


def _silu(x):
    return x * jax.nn.sigmoid(x)


def _mod_body(c_ref, w_ref, b_ref, o_ref):
    a = _silu(c_ref[...]).astype(BF16)
    w = w_ref[0].astype(BF16)
    o_ref[0] = jnp.dot(a, w, preferred_element_type=F32) + b_ref[0]


def _mod_call(c8, w_mod, b_mod):
    depth, d, n6 = w_mod.shape
    tn = 1024
    return pl.pallas_call(
        _mod_body,
        out_shape=jax.ShapeDtypeStruct((depth, SUBLANES, n6), F32),
        grid=(depth, n6 // tn),
        in_specs=[pl.BlockSpec((SUBLANES, d), lambda l, j: (0, 0)),
                  pl.BlockSpec((1, d, tn), lambda l, j: (l, 0, j)),
                  pl.BlockSpec((1, 1, tn), lambda l, j: (l, 0, j))],
        out_specs=pl.BlockSpec((1, SUBLANES, tn), lambda l, j: (l, 0, j)),
        compiler_params=_params(("arbitrary", "arbitrary"), 40),
        name="mod",
    )(c8, w_mod, b_mod.reshape(depth, 1, n6))


def _rms_modulate(x, g, sc, sh):
    ms = jnp.mean(x * x, axis=-1, keepdims=True)
    return (x * lax.rsqrt(ms + EPS)) * (g * (1.0 + sc)) + sh


def _inproj_body(x_ref, sh_ref, sc_ref, g_ref, wT_ref, wu_ref, gq_ref, gk_ref, cos_ref, sin_ref,
                 qT_ref, k_ref, vT_ref, u_ref, *, rope, q_scale):
    tm = x_ref.shape[1]
    aw = N_HEADS * V_DIM
    h = _rms_modulate(x_ref[0], g_ref[...], sc_ref[0], sh_ref[0])
    hb = h.astype(BF16)
    pT = lax.dot_general(wT_ref[...], hb, NT_DIMS, preferred_element_type=F32)
    u_ref[0] = jnp.dot(hb, wu_ref[...], preferred_element_type=F32)

    def norm_rope(t, g):
        t3 = t.reshape(2 * N_HEADS, HEAD_DIM, tm)
        ms = jnp.mean(t3 * t3, axis=1, keepdims=True)
        y = (t3 * lax.rsqrt(ms + EPS)) * g[None]
        if rope:
            q4 = HEAD_DIM // 4
            sw = jnp.concatenate([y[:, q4:2 * q4], y[:, 0:q4], y[:, 3 * q4:], y[:, 2 * q4:3 * q4]], axis=1)
            y = y * cos_ref[...][None] + sw * sin_ref[...][None]
        return y.reshape(N_HEADS, V_DIM, tm)

    q = norm_rope(pT[0:aw], gq_ref[...]) * q_scale
    qT_ref[0] = q.astype(BF16)
    k = norm_rope(pT[aw:2 * aw], gk_ref[...])
    for hh in range(N_HEADS):
        k_ref[0, hh] = k[hh].T.astype(BF16)
    vT_ref[0, :, 0, 0:V_DIM] = pT[2 * aw:3 * aw].reshape(N_HEADS, V_DIM, tm).astype(BF16)
    vT_ref[0, :, 0, V_DIM:] = jnp.ones((N_HEADS, V_ROWS - V_DIM, tm), BF16)


def _inproj_call(x, sh, sc, g, wT, wu, gq, gk, cosT, sinT, *, rope):
    b, n, d = x.shape
    tm = min(TOKEN_TILE, n)
    aw = N_HEADS * V_DIM
    pw = wu.shape[1]
    body = functools.partial(_inproj_body, rope=rope, q_scale=HEAD_DIM ** -0.5 * LOG2_E)
    vec = lambda bb, i: (bb, 0, 0)
    return pl.pallas_call(
        body,
        out_shape=(jax.ShapeDtypeStruct((b, N_HEADS, V_DIM, n), BF16),
                   jax.ShapeDtypeStruct((b, N_HEADS, n, V_DIM), BF16),
                   jax.ShapeDtypeStruct((b, N_HEADS, n // tm, V_ROWS, tm), BF16),
                   jax.ShapeDtypeStruct((b, n, pw), F32)),
        grid=(b, n // tm),
        in_specs=[pl.BlockSpec((1, tm, d), lambda bb, i: (bb, i, 0)),
                  pl.BlockSpec((1, 1, d), vec), pl.BlockSpec((1, 1, d), vec),
                  _const_spec((1, d)), _const_spec((3 * aw, d)), _const_spec((d, pw)),
                  _const_spec((HEAD_DIM, 1)), _const_spec((HEAD_DIM, 1)),
                  pl.BlockSpec((HEAD_DIM, tm), lambda bb, i: (0, i)),
                  pl.BlockSpec((HEAD_DIM, tm), lambda bb, i: (0, i))],
        out_specs=(pl.BlockSpec((1, N_HEADS, V_DIM, tm), lambda bb, i: (bb, 0, 0, i)),
                   pl.BlockSpec((1, N_HEADS, tm, V_DIM), lambda bb, i: (bb, 0, i, 0)),
                   pl.BlockSpec((1, N_HEADS, 1, V_ROWS, tm), lambda bb, i: (bb, 0, i, 0, 0)),
                   pl.BlockSpec((1, tm, pw), lambda bb, i: (bb, i, 0))),
        compiler_params=_params(("arbitrary", "arbitrary"), 56),
        name="inproj_rope" if rope else "inproj_ctx",
    )(x, sh, sc, g, wT, wu, gq, gk, cosT, sinT)


def _attn_body(*refs, n_key_tiles, key_tile, has_ctx, lam_init):
    if has_ctx:
        lam_ref, qT_ref, k_ref, vT_ref, kc_ref, vcT_ref, gs_ref, o_ref, qbd_ref, *bufs = refs
    else:
        lam_ref, qT_ref, k_ref, vT_ref, gs_ref, o_ref, qbd_ref, *bufs = refs
    s_bufs, e_bufs, acc_ref = bufs[0:2], bufs[2:4], bufs[4]
    tq = qT_ref.shape[-1]
    q = qT_ref[0, 0]
    z = jnp.zeros((HEAD_DIM, tq), BF16)
    qbd_ref[0:HEAD_DIM, 0:tq] = q[0:HEAD_DIM]
    qbd_ref[0:HEAD_DIM, tq:] = z
    qbd_ref[HEAD_DIM:, 0:tq] = z
    qbd_ref[HEAD_DIM:, tq:] = q[HEAD_DIM:]

    def scores(kt, slot):
        s = jnp.dot(kt, qbd_ref[...], preferred_element_type=F32)
        s_bufs[slot][0:kt.shape[0]] = s
        return jnp.max(s, axis=0, keepdims=True)

    def softmax(rows, slot, mx, m):
        m_new = jnp.maximum(m, mx)
        alpha = jnp.exp2(m - m_new)
        e_bufs[slot][0:rows] = jnp.exp2((s_bufs[slot][0:rows] - m_new).astype(BF16))
        return alpha, m_new

    def pv(rows, slot, alpha, vt):
        acc_ref[...] = alpha * acc_ref[...] + jnp.dot(vt, e_bufs[slot][0:rows], preferred_element_type=F32)

    def k_tile(j):
        return k_ref[0, 0, pl.ds(pl.multiple_of(j * key_tile, key_tile), key_tile), :]

    tk = key_tile
    m = jnp.full((1, 2 * tq), -jnp.inf, F32)
    acc_ref[...] = jnp.zeros(acc_ref.shape, F32)
    if n_key_tiles == 1:
        mx_c = scores(k_tile(0), 1)
    else:
        mx_p = scores(k_tile(0), 0)
        mx_c = scores(k_tile(1), 1)
        a_p, m = softmax(tk, 0, mx_p, m)

        unroll = max(u for u in (2, 4, 6) if (n_key_tiles - 2) % u == 0)

        def body(t, carry):
            mx_c, a_p, m = carry
            for u in range(unroll):
                i = 2 + unroll * t + u
                mx_n = scores(k_tile(i), u % 2)
                a_c, m = softmax(tk, 1 - u % 2, mx_c, m)
                pv(tk, u % 2, a_p, vT_ref[0, 0, i - 2])
                mx_c, a_p = mx_n, a_c
            return mx_c, a_p, m

        mx_c, a_p, m = lax.fori_loop(0, (n_key_tiles - 2) // unroll, body, (mx_c, a_p, m))
    if has_ctx:
        cl = kc_ref.shape[2]
        mx_x = scores(kc_ref[0, 0], 0)
    a_c, m = softmax(tk, 1, mx_c, m)
    if n_key_tiles > 1:
        pv(tk, 0, a_p, vT_ref[0, 0, n_key_tiles - 2])
    if has_ctx:
        a_x, m = softmax(cl, 0, mx_x, m)
    pv(tk, 1, a_c, vT_ref[0, 0, n_key_tiles - 1])
    if has_ctx:
        pv(cl, 0, a_x, vcT_ref[0, 0, 0])
    acc = acc_ref[0:V_DIM]
    l = acc_ref[V_DIM:V_DIM + 1]

    lv = lam_ref[...]
    lam = (jnp.exp(jnp.sum(lv[0:1] * lv[1:2], axis=1, keepdims=True))
           - jnp.exp(jnp.sum(lv[2:3] * lv[3:4], axis=1, keepdims=True)) + lam_init)
    oT =acc[:, 0:tq] / l[:, 0:tq] - lam * (acc[:, tq:] / l[:, tq:])
    ms = jnp.mean(oT * oT, axis=0, keepdims=True)
    on = (oT * lax.rsqrt(ms + EPS)) * gs_ref[...] * (1.0 - lam_init)
    o_ref[0] = on.T.astype(BF16)


def _attn_call(lam_qk, qT, k, vT, kc, vcT, gs, *, lam_init):
    b, nh, _, n = qT.shape
    nk = k.shape[2]
    n_key_tiles, key_tile = vT.shape[2], vT.shape[4]
    tq = min(Q_TILE, n)
    has_ctx = kc is not None
    assert n_key_tiles == 1 or n_key_tiles % 2 == 0
    assert not has_ctx or kc.shape[2] <= key_tile
    body = functools.partial(_attn_body, n_key_tiles=n_key_tiles, key_tile=key_tile,
                             has_ctx=has_ctx, lam_init=lam_init)
    in_specs = [_const_spec((4, HEAD_DIM)),
                pl.BlockSpec((1, 1, V_DIM, tq), lambda bb, hh, i: (bb, hh, 0, i)),
                pl.BlockSpec((1, 1, nk, V_DIM), lambda bb, hh, i: (bb, hh, 0, 0)),
                pl.BlockSpec((1, 1, n_key_tiles, V_ROWS, key_tile), lambda bb, hh, i: (bb, hh, 0, 0, 0))]
    args = [lam_qk, qT, k, vT]
    if has_ctx:
        cl = kc.shape[2]
        in_specs += [pl.BlockSpec((1, 1, cl, V_DIM), lambda bb, hh, i: (bb, hh, 0, 0)),
                     pl.BlockSpec((1, 1, 1, V_ROWS, cl), lambda bb, hh, i: (bb, hh, 0, 0, 0))]
        args += [kc, vcT]
    in_specs.append(_const_spec((V_DIM, 1)))
    args.append(gs)
    return pl.pallas_call(
        body,
        out_shape=jax.ShapeDtypeStruct((b, n, nh * V_DIM), BF16),
        grid=(b, nh, n // tq),
        in_specs=in_specs,
        out_specs=pl.BlockSpec((1, tq, V_DIM), lambda bb, hh, i: (bb, i, hh)),
        scratch_shapes=[pltpu.VMEM((V_DIM, 2 * tq), BF16),
                        pltpu.VMEM((key_tile, 2 * tq), F32), pltpu.VMEM((key_tile, 2 * tq), F32),
                        pltpu.VMEM((key_tile, 2 * tq), BF16), pltpu.VMEM((key_tile, 2 * tq), BF16),
                        pltpu.VMEM((V_ROWS, 2 * tq), F32)],
        compiler_params=_params(("arbitrary", "arbitrary", "arbitrary"), 48),
        name="attn_latent" if has_ctx else "attn_ctx",
    )(*args)


def _outproj_body(o_ref, u_ref, up_ref, un_ref, x_ref, g1_ref, wp_ref, ps_ref, wo_ref,
                  xo_ref, ubuf, cat, *, n):
    tm = u_ref.shape[1]
    aw = o_ref.shape[2]
    pg = wp_ref.shape[1]
    i = pl.program_id(1)
    last = pl.num_programs(1) - 1
    ubuf[0:HALO] = jnp.where(i > 0, up_ref[0], 0.0)
    ubuf[HALO:HALO + tm] = u_ref[0]
    ubuf[HALO + tm:] = jnp.where(i < last, un_ref[0], 0.0)
    t = i * tm + lax.broadcasted_iota(jnp.int32, (tm, 1), 0)
    cat[:, 0:aw] = o_ref[0]
    for g, w in enumerate(POOL_WINDOWS):
        half = w // 2
        c0 = g * pg
        acc = ubuf[HALO - half:HALO - half + tm, c0:c0 + pg]
        for s in range(-half + 1, w - half):
            acc = acc + ubuf[HALO + s:HALO + s + tm, c0:c0 + pg]
        lo = jnp.maximum(t - half, 0)
        hi = jnp.minimum(t - half + w - 1, n - 1)
        cnt = (hi - lo + 1).astype(F32)
        dlt = acc / cnt - ubuf[HALO:HALO + tm, c0:c0 + pg]
        y = jnp.dot(dlt.astype(BF16), wp_ref[g], preferred_element_type=F32) * ps_ref[:, c0:c0 + pg]
        cat[:, aw + c0:aw + c0 + pg] = y.astype(BF16)
    mix = jnp.dot(cat[...], wo_ref[...], preferred_element_type=F32)
    xo_ref[0] = x_ref[0] + g1_ref[0] * mix


def _outproj_call(o, u, x, g1, wp, ps, wo):
    b, n, d = x.shape
    tm = min(TOKEN_TILE, n)
    aw, pw = o.shape[2], u.shape[2]
    hb = tm // HALO
    nhalo = n // HALO
    body = functools.partial(_outproj_body, n=n)
    return pl.pallas_call(
        body,
        out_shape=jax.ShapeDtypeStruct((b, n, d), F32),
        grid=(b, n // tm),
        in_specs=[pl.BlockSpec((1, tm, aw), lambda bb, i: (bb, i, 0)),
                  pl.BlockSpec((1, tm, pw), lambda bb, i: (bb, i, 0)),
                  pl.BlockSpec((1, HALO, pw), lambda bb, i: (bb, jnp.maximum(i * hb - 1, 0), 0)),
                  pl.BlockSpec((1, HALO, pw), lambda bb, i: (bb, jnp.minimum((i + 1) * hb, nhalo - 1), 0)),
                  pl.BlockSpec((1, tm, d), lambda bb, i: (bb, i, 0)),
                  pl.BlockSpec((1, 1, d), lambda bb, i: (bb, 0, 0)),
                  _const_spec(wp.shape), _const_spec((1, pw)), _const_spec(wo.shape)],
        out_specs=pl.BlockSpec((1, tm, d), lambda bb, i: (bb, i, 0)),
        scratch_shapes=[pltpu.VMEM((tm + 2 * HALO, pw), F32),
                        pltpu.VMEM((tm, aw + pw), BF16)],
        compiler_params=_params(("arbitrary", "arbitrary"), 48),
        name="outproj",
    )(o, u, u, u, x, g1, wp, ps, wo)


def _first_argmax(vals):
    best = vals[0]
    idx = jnp.zeros(best.shape, jnp.int32)
    for j in range(1, len(vals)):
        c = vals[j] > best
        best = jnp.where(c, vals[j], best)
        idx = jnp.where(c, j, idx)
    return best, idx


def _ln2_body(x_ref, sh_ref, sc_ref, g_ref, wr_ref, br_ref, *rest):
    h_ref, e_ref, gcol_ref = rest[-3:]
    tm = x_ref.shape[0]
    h = _rms_modulate(x_ref[...], g_ref[...], sc_ref[0], sh_ref[0])
    h_ref[...] = h
    hh = h.astype(BF16)
    hl = (h - hh.astype(F32)).astype(BF16)
    wr = wr_ref[...]
    wh = wr.astype(BF16)
    wl = (wr - wh.astype(F32)).astype(BF16)
    dg = lambda a, bb: lax.dot_general(a, bb, NT_DIMS, preferred_element_type=F32)
    lg = dg(wh, hh) + dg(wh, hl) + dg(wl, hh) + br_ref[...]
    mx = jnp.max(lg, axis=0, keepdims=True)
    ex = jnp.exp(lg - mx)
    p = ex / jnp.sum(ex, axis=0, keepdims=True)
    rows = [p[e:e + 1] for e in range(N_EXPERTS)]
    scores = []
    for g in range(N_EXPERT_GROUPS):
        r = rows[g * EXPERTS_PER_GROUP:(g + 1) * EXPERTS_PER_GROUP]
        best = None
        for a in range(EXPERTS_PER_GROUP):
            for bb in range(a + 1, EXPERTS_PER_GROUP):
                s = r[a] + r[bb]
                best = s if best is None else jnp.maximum(best, s)
        scores.append(best)
    _, gsel = _first_argmax(scores)
    vals = []
    for j in range(EXPERTS_PER_GROUP):
        v = rows[j]
        for g in range(1, N_EXPERT_GROUPS):
            v = jnp.where(gsel == g, rows[g * EXPERTS_PER_GROUP + j], v)
        vals.append(v)
    v1, i1 = _first_argmax(vals)
    v2 = jnp.full(v1.shape, -jnp.inf, F32)
    i2 = jnp.zeros(i1.shape, jnp.int32)
    for j in range(EXPERTS_PER_GROUP):
        c = jnp.logical_and(i1 != j, vals[j] > v2)
        v2 = jnp.where(c, vals[j], v2)
        i2 = jnp.where(c, j, i2)
    den = v1 + v2
    e1 = gsel * EXPERTS_PER_GROUP + i1
    e2 = gsel * EXPERTS_PER_GROUP + i2
    row2 = lax.broadcasted_iota(jnp.int32, (2, tm), 0)
    e_ref[...] = jnp.where(row2 == 0, e1, e2)
    rowg = lax.broadcasted_iota(jnp.int32, (LANES, tm), 0)
    gates = jnp.where(rowg == 0, v1 / den, jnp.where(rowg == 1, v2 / den, 0.0))
    gcol_ref[...] = gates.T


def _ln2_call(x, sh, sc, g, wr, br, *, total, offset=0, bufs=None):
    b, n, d = x.shape
    tm = min(TOKEN_TILE, n)
    per = n // tm
    t = b * n
    off = offset // tm
    nt = t // tm
    steps = nt if bufs is not None else total // tm
    assert total % tm == 0 and offset % tm == 0
    src = lambda i: jnp.minimum(i, nt - 1)
    in_specs = [pl.BlockSpec((tm, d), lambda i: (src(i), 0)),
                pl.BlockSpec((1, 1, d), lambda i: (src(i) // per, 0, 0)),
                pl.BlockSpec((1, 1, d), lambda i: (src(i) // per, 0, 0)),
                _const_spec((1, d)), _const_spec(wr.shape), _const_spec(br.shape)]
    args = [x.reshape(t, d), sh, sc, g, wr, br]
    aliases = {}
    if bufs is not None:
        in_specs += [pl.BlockSpec(memory_space=pl.ANY)] * 3
        aliases = {len(args) + j: j for j in range(3)}
        args += list(bufs)
    return pl.pallas_call(
        _ln2_body,
        out_shape=(jax.ShapeDtypeStruct((total, d), F32),
                   jax.ShapeDtypeStruct((2, total), jnp.int32),
                   jax.ShapeDtypeStruct((total, LANES), F32)),
        grid=(steps,),
        in_specs=in_specs,
        out_specs=(pl.BlockSpec((tm, d), lambda i: (i + off, 0)),
                   pl.BlockSpec((2, tm), lambda i: (0, i + off)),
                   pl.BlockSpec((tm, LANES), lambda i: (i + off, 0))),
        input_output_aliases=aliases,
        compiler_params=_params(("arbitrary",), 40),
        name="ln2_router",
    )(*args)


def _dispatch_body(e_ref, dest_ref, cnt_ref, cnt_sc, carry_sc, start_sc, *, rows):
    ph = pl.program_id(0)
    j = pl.program_id(1)
    tt = e_ref.shape[1]
    e = e_ref[...]
    eid = lax.broadcasted_iota(jnp.int32, (N_EXPERTS, tt), 0)
    oh0 = (eid == e[0:1]).astype(F32)
    oh1 = (eid == e[1:2]).astype(F32)
    tot0 = jnp.sum(oh0, axis=1, keepdims=True)
    tot1 = jnp.sum(oh1, axis=1, keepdims=True)

    @pl.when(jnp.logical_and(ph == 0, j == 0))
    def _():
        cnt_sc[...] = jnp.zeros(cnt_sc.shape, F32)

    @pl.when(ph == 0)
    def _():
        cnt_sc[...] += tot0 + tot1

    @pl.when(jnp.logical_and(ph == 1, j == 0))
    def _():
        cnt = cnt_sc[...]
        padded = jnp.ceil(cnt / rows) * rows
        ecol = lax.broadcasted_iota(jnp.int32, (N_EXPERTS, 1), 0)
        start = jnp.zeros((N_EXPERTS, 1), F32)
        for ee in range(N_EXPERTS - 1):
            start = start + jnp.where(ecol > ee, padded[ee:ee + 1], 0.0)
        start_sc[...] = start
        carry_sc[...] = jnp.zeros(carry_sc.shape, F32)
        cnt_ref[...] = cnt

    @pl.when(ph == 1)
    def _():
        r = lax.broadcasted_iota(jnp.int32, (tt, tt), 0)
        c = lax.broadcasted_iota(jnp.int32, (tt, tt), 1)
        upper = (r < c).astype(BF16)
        pre0 = jnp.dot(oh0.astype(BF16), upper, preferred_element_type=F32)
        pre1 = jnp.dot(oh1.astype(BF16), upper, preferred_element_type=F32)
        base = start_sc[...] + carry_sc[...]
        d0 = jnp.sum(oh0 * (base + pre0), axis=0, keepdims=True)
        d1 = jnp.sum(oh1 * (base + tot0 + pre1), axis=0, keepdims=True)
        row2 = lax.broadcasted_iota(jnp.int32, (2, tt), 0)
        dest_ref[...] = jnp.where(row2 == 0, d0, d1).astype(jnp.int32)
        carry_sc[...] += tot0 + tot1


def _dispatch_call(experts, *, rows):
    t = experts.shape[1]
    tt = TOKEN_TILE
    body = functools.partial(_dispatch_body, rows=rows)
    return pl.pallas_call(
        body,
        out_shape=(jax.ShapeDtypeStruct((2, t), jnp.int32),
                   jax.ShapeDtypeStruct((N_EXPERTS, 1), F32)),
        grid=(2, t // tt),
        in_specs=[pl.BlockSpec((2, tt), lambda ph, j: (0, j))],
        out_specs=(pl.BlockSpec((2, tt), lambda ph, j: (0, j * ph)),
                   pl.BlockSpec((N_EXPERTS, 1), lambda ph, j: (0, 0))),
        scratch_shapes=[pltpu.VMEM((N_EXPERTS, 1), F32)] * 3,
        compiler_params=_params(("arbitrary", "arbitrary"), 32),
        name="dispatch",
    )(experts)


def _row_copy(src, src_row, dst, dst_row, sem):
    return pltpu.make_async_copy(src.at[pl.ds(src_row, 1)], dst.at[pl.ds(dst_row, 1)], sem)


def _scatter_body(dest_ref, pend_ref, h_ref, xs_ref, zbuf, sem, zsem, *, n_tokens, rows):
    tm = h_ref.shape[0]
    base = pl.program_id(0) * tm

    n_rows = xs_ref.shape[0]
    used = pend_ref[N_EXPERTS - 1]

    def zero_copy(start):
        return pltpu.make_async_copy(zbuf, xs_ref.at[pl.ds(pl.multiple_of(start, rows), rows)], zsem)

    blocks = [(pend_ref[e] > (pend_ref[e - 1] if e else 0), pend_ref[e] - rows) for e in range(N_EXPERTS)]
    blocks += [(used + j * rows < n_rows, used + j * rows) for j in range(N_EXPERTS)]

    @pl.when(pl.program_id(0) == 0)
    def _():
        zbuf[...] = jnp.zeros(zbuf.shape, zbuf.dtype)
        for cond, start in blocks:
            pl.when(cond)(lambda start=start: zero_copy(start).start())
        for cond, start in blocks:
            pl.when(cond)(lambda start=start: zero_copy(start).wait())

    def issue(r, carry):
        for kk in range(2):
            _row_copy(h_ref, r, xs_ref, dest_ref[kk * n_tokens + base + r], sem).start()
        return carry

    lax.fori_loop(0, tm, issue, 0, unroll=DMA_ISSUE_UNROLL)
    for kk in range(2):
        pltpu.make_async_copy(h_ref, xs_ref.at[pl.ds(0, tm)], sem).wait()


def _scatter_call(dest_flat, pad_ends, tokens, *, n_rows, rows):
    t, d = tokens.shape
    tm = TOKEN_TILE
    body = functools.partial(_scatter_body, n_tokens=t, rows=rows)
    return pl.pallas_call(
        body,
        out_shape=jax.ShapeDtypeStruct((n_rows, d), tokens.dtype),
        grid_spec=pltpu.PrefetchScalarGridSpec(
            num_scalar_prefetch=2,
            grid=(t // tm,),
            in_specs=[pl.BlockSpec((tm, d), lambda i, dst, pe: (i, 0))],
            out_specs=pl.BlockSpec(memory_space=pl.ANY),
            scratch_shapes=[pltpu.VMEM((rows, d), tokens.dtype),
                            pltpu.SemaphoreType.DMA(()), pltpu.SemaphoreType.DMA(())]),
        compiler_params=_params(("arbitrary",), 32),
        name="moe_scatter",
    )(dest_flat, pad_ends, tokens)


def _ffn_body(be_ref, nu_ref, xs_ref, wgT_ref, wuT_ref, wdT_ref, ys_ref):
    i = pl.program_id(0)

    @pl.when(i < nu_ref[0])
    def _():
        xb = xs_ref[...].astype(BF16)
        aT = lax.dot_general(wgT_ref[0, 0], xb, NT_DIMS, preferred_element_type=F32)
        bT = lax.dot_general(wuT_ref[0, 0], xb, NT_DIMS, preferred_element_type=F32)
        hT = (_silu(aT) * bT).astype(BF16)
        yT = jnp.dot(wdT_ref[0, 0], hT, preferred_element_type=F32)
        ys_ref[...] = yT.T

    @pl.when(i >= nu_ref[0])
    def _():
        ys_ref[...] = jnp.zeros(ys_ref.shape, F32)


def _ffn_call(block_e, n_used, xs, wgT, wuT, wdT, *, rows, layer):
    p, d = xs.shape
    de = wgT.shape[2]
    return pl.pallas_call(
        _ffn_body,
        out_shape=jax.ShapeDtypeStruct((p, d), F32),
        grid_spec=pltpu.PrefetchScalarGridSpec(
            num_scalar_prefetch=2,
            grid=(p // rows,),
            in_specs=[pl.BlockSpec((rows, d), lambda i, be, nu: (jnp.minimum(i, nu[0] - 1), 0)),
                      pl.BlockSpec((1, 1, de, d), lambda i, be, nu: (layer, be[i], 0, 0)),
                      pl.BlockSpec((1, 1, de, d), lambda i, be, nu: (layer, be[i], 0, 0)),
                      pl.BlockSpec((1, 1, d, de), lambda i, be, nu: (layer, be[i], 0, 0))],
            out_specs=pl.BlockSpec((rows, d), lambda i, be, nu: (i, 0))),
        compiler_params=_params(("arbitrary",), 56),
        name="moe_ffn",
    )(block_e, n_used, xs, wgT, wuT, wdT)


def _combine_body(dest_ref, x_ref, gcol_ref, g2_ref, ys_ref, xo_ref, ya, yb, sem, *, n_tokens, offset):
    tm = x_ref.shape[0]
    base = offset + pl.program_id(0) * tm

    def issue(r, carry):
        _row_copy(ys_ref, dest_ref[base + r], ya, r, sem).start()
        _row_copy(ys_ref, dest_ref[n_tokens + base + r], yb, r, sem).start()
        return carry

    lax.fori_loop(0, tm, issue, 0, unroll=DMA_ISSUE_UNROLL)
    pltpu.make_async_copy(ys_ref.at[pl.ds(0, tm)], ya, sem).wait()
    pltpu.make_async_copy(ys_ref.at[pl.ds(0, tm)], yb, sem).wait()
    gc = gcol_ref[...]
    y = gc[:, 0:1] * ya[...] + gc[:, 1:2] * yb[...]
    xo_ref[...] = x_ref[...] + g2_ref[0] * y


def _combine_call(dest_flat, x, gcol, g2, ys, *, n_tokens, offset):
    b, n, d = x.shape
    tm = min(TOKEN_TILE, n)
    per = n // tm
    t = b * n
    off_blocks = offset // tm
    body = functools.partial(_combine_body, n_tokens=n_tokens, offset=offset)
    out = pl.pallas_call(
        body,
        out_shape=jax.ShapeDtypeStruct((t, d), F32),
        grid_spec=pltpu.PrefetchScalarGridSpec(
            num_scalar_prefetch=1,
            grid=(t // tm,),
            in_specs=[pl.BlockSpec((tm, d), lambda i, dst: (i, 0)),
                      pl.BlockSpec((tm, LANES), lambda i, dst: (i + off_blocks, 0)),
                      pl.BlockSpec((1, 1, d), lambda i, dst: (i // per, 0, 0)),
                      pl.BlockSpec(memory_space=pl.ANY)],
            out_specs=pl.BlockSpec((tm, d), lambda i, dst: (i, 0)),
            scratch_shapes=[pltpu.VMEM((tm, d), F32), pltpu.VMEM((tm, d), F32),
                            pltpu.SemaphoreType.DMA(())]),
        compiler_params=_params(("arbitrary",), 40),
        name="moe_combine",
    )(dest_flat, x.reshape(t, d), gcol, g2, ys)
    return out.reshape(b, n, d)


def _rope_tables(n):
    rows = n // GRID_W
    row = jnp.repeat(jnp.arange(rows), GRID_W).astype(F32)
    col = jnp.tile(jnp.arange(GRID_W), rows).astype(F32)
    nf = HEAD_DIM // 4
    inv = ROPE_THETA ** (-jnp.arange(nf, dtype=F32) / nf)
    ar = (row[:, None] * inv).T
    ac = (col[:, None] * inv).T
    cosT = jnp.concatenate([jnp.cos(ar), jnp.cos(ar), jnp.cos(ac), jnp.cos(ac)], axis=0)
    sinT = jnp.concatenate([-jnp.sin(ar), jnp.sin(ar), -jnp.sin(ac), jnp.sin(ac)], axis=0)
    return cosT, sinT


def _moe(tokens, experts, w_gateT, w_upT, w_downT, layer):
    t, d = tokens.shape
    rows = MOE_ROWS
    n_blocks = -(-2 * t // rows) + N_EXPERTS
    dest, counts = _dispatch_call(experts, rows=rows)
    cnt = counts[:, 0].astype(jnp.int32)
    blocks = (cnt + rows - 1) // rows
    ends = jnp.cumsum(blocks)
    n_used = ends[-1]
    bid = jnp.arange(n_blocks, dtype=jnp.int32)
    block_e = jnp.sum(jnp.minimum(bid, n_used - 1)[:, None] >= ends[None, :], axis=1).astype(jnp.int32)
    block_e = jnp.minimum(block_e, N_EXPERTS - 1)
    dest_flat = dest.reshape(2 * t)
    xs = _scatter_call(dest_flat, (ends * rows).astype(jnp.int32), tokens, n_rows=n_blocks * rows, rows=rows)
    ys = _ffn_call(block_e, n_used.reshape(1).astype(jnp.int32), xs, w_gateT, w_upT, w_downT,
                   rows=rows, layer=layer)
    return ys, dest_flat


def kernel(x, c, ctx, c_ctx, w_mod, b_mod, norm1, norm2, w_in, q_norm, k_norm, lam_qk, sub_norm, w_pool,
           pool_scale, w_out, w_router, b_router, w_gate, w_up, w_down):
    b, n, d = x.shape
    cl = ctx.shape[1]
    depth = w_mod.shape[0]
    aw = N_HEADS * V_DIM
    assert n % TOKEN_TILE == 0 and n % GRID_W == 0 and cl % SUBLANES == 0
    assert (b * n) % TOKEN_TILE == 0 and (b * cl) % TOKEN_TILE == 0

    c8 = jnp.concatenate([c, c_ctx[None], jnp.zeros((SUBLANES - b - 1, d), F32)], axis=0)
    mod = _mod_call(c8, w_mod, b_mod)
    cosT, sinT = _rope_tables(n)
    zc = jnp.zeros((HEAD_DIM, cl), F32)
    wrT = w_router.T
    br = b_router.reshape(N_EXPERTS, 1)
    wgT = jnp.swapaxes(w_gate, 2, 3).astype(BF16)
    wuT = jnp.swapaxes(w_up, 2, 3).astype(BF16)
    wdT = jnp.swapaxes(w_down, 2, 3).astype(BF16)

    xc = ctx
    for li in range(depth):
        last = li == depth - 1
        lam_init = 0.8 - 0.6 * math.exp(-0.3 * li)
        m6 = mod[li].reshape(SUBLANES, 6, d)
        lat = [m6[0:b, j].reshape(b, 1, d) for j in range(6)]
        cxm = [jnp.broadcast_to(m6[b, j].reshape(1, 1, d), (b, 1, d)) for j in range(6)]

        wi = w_in[li]
        wT = wi[:, :3 * aw].T.astype(BF16)
        wu = wi[:, 3 * aw:].astype(BF16)
        g1n = norm1[li].reshape(1, d)
        g2n = norm2[li].reshape(1, d)
        gq = q_norm[li].reshape(HEAD_DIM, 1)
        gk = k_norm[li].reshape(HEAD_DIM, 1)
        gs = sub_norm[li].reshape(V_DIM, 1)
        wp = w_pool[li].astype(BF16)
        ps = pool_scale[li].reshape(1, -1)
        wo = w_out[li].astype(BF16)

        qT, k, vT, u = _inproj_call(x, lat[0], lat[1], g1n, wT, wu, gq, gk, cosT, sinT, rope=True)
        qcT, kc, vcT, uc = _inproj_call(xc, cxm[0], cxm[1], g1n, wT, wu, gq, gk, zc, zc, rope=False)
        o = _attn_call(lam_qk[li], qT, k, vT, kc, vcT, gs, lam_init=lam_init)
        x = _outproj_call(o, u, x, lat[2], wp, ps, wo)
        if not last:
            oc = _attn_call(lam_qk[li], qcT, kc, vcT, None, None, gs, lam_init=lam_init)
            xc = _outproj_call(oc, uc, xc, cxm[2], wp, ps, wo)

        n_tok = b * n if last else b * (n + cl)
        tok, ex, gcol = _ln2_call(x, lat[3], lat[4], g2n, wrT, br, total=n_tok)
        if not last:
            tok, ex, gcol = _ln2_call(xc, cxm[3], cxm[4], g2n, wrT, br, total=n_tok, offset=b * n,
                                      bufs=(tok, ex, gcol))
        ys, dest_flat = _moe(tok, ex, wgT, wuT, wdT, li)
        x = _combine_call(dest_flat, x, gcol, lat[5], ys, n_tokens=n_tok, offset=0)
        if not last:
            xc = _combine_call(dest_flat, xc, gcol, cxm[5], ys, n_tokens=n_tok, offset=b * n)
    return x
```

```python
import functools
import math

import jax
import jax.numpy as jnp
from jax import lax
from jax.experimental import pallas as pl
from jax.experimental.pallas import tpu as pltpu

N_HEADS = 8
HEAD_DIM = 64
V_DIM = 2 * HEAD_DIM
V_ROWS = V_DIM + 16
POOL_WINDOWS = (2, 4, 8, 16)
N_EXPERTS = 16
N_EXPERT_GROUPS = 4
EXPERTS_PER_GROUP = N_EXPERTS // N_EXPERT_GROUPS
GRID_W = 64
ROPE_THETA = 10000.0
EPS = 1e-6

V7X_VMEM_BYTES = 64 * 1024 * 1024
SUBLANES = 8
LANES = 128

TOKEN_TILE = 512
Q_TILE = 1024
DMA_ISSUE_UNROLL = 8
MOE_ROWS = 256
HALO = 8

F32 = jnp.float32
BF16 = jnp.bfloat16
NT_DIMS = (((1,), (1,)), ((), ()))
LOG2_E = math.log2(math.e)


def _params(semantics, vmem_mb):
    return pltpu.CompilerParams(dimension_semantics=semantics,
                                vmem_limit_bytes=min(vmem_mb * 1024 * 1024, V7X_VMEM_BYTES - (4 << 20)))


def _const_spec(shape):
    nd = len(shape)
    return pl.BlockSpec(shape, lambda *_: (0,) * nd, pipeline_mode=pl.Buffered(1))


def _silu(x):
    return x * jax.nn.sigmoid(x)


def _mod_body(c_ref, w_ref, b_ref, o_ref):
    a = _silu(c_ref[...]).astype(BF16)
    w = w_ref[0].astype(BF16)
    o_ref[0] = jnp.dot(a, w, preferred_element_type=F32) + b_ref[0]


def _mod_call(c8, w_mod, b_mod):
    depth, d, n6 = w_mod.shape
    tn = 1024
    return pl.pallas_call(
        _mod_body,
        out_shape=jax.ShapeDtypeStruct((depth, SUBLANES, n6), F32),
        grid=(depth, n6 // tn),
        in_specs=[pl.BlockSpec((SUBLANES, d), lambda l, j: (0, 0)),
                  pl.BlockSpec((1, d, tn), lambda l, j: (l, 0, j)),
                  pl.BlockSpec((1, 1, tn), lambda l, j: (l, 0, j))],
        out_specs=pl.BlockSpec((1, SUBLANES, tn), lambda l, j: (l, 0, j)),
        compiler_params=_params(("arbitrary", "arbitrary"), 40),
        name="mod",
    )(c8, w_mod, b_mod.reshape(depth, 1, n6))


def _rms_modulate(x, g, sc, sh):
    ms = jnp.mean(x * x, axis=-1, keepdims=True)
    return (x * lax.rsqrt(ms + EPS)) * (g * (1.0 + sc)) + sh


def _inproj_body(x_ref, sh_ref, sc_ref, g_ref, wT_ref, wu_ref, gq_ref, gk_ref, cos_ref, sin_ref,
                 qT_ref, k_ref, vT_ref, u_ref, *, rope, q_scale):
    tm = x_ref.shape[1]
    aw = N_HEADS * V_DIM
    h = _rms_modulate(x_ref[0], g_ref[...], sc_ref[0], sh_ref[0])
    hb = h.astype(BF16)
    pT = lax.dot_general(wT_ref[...], hb, NT_DIMS, preferred_element_type=F32)
    u_ref[0] = jnp.dot(hb, wu_ref[...], preferred_element_type=F32)

    def norm_rope(t, g):
        t3 = t.reshape(2 * N_HEADS, HEAD_DIM, tm)
        ms = jnp.mean(t3 * t3, axis=1, keepdims=True)
        y = (t3 * lax.rsqrt(ms + EPS)) * g[None]
        if rope:
            q4 = HEAD_DIM // 4
            sw = jnp.concatenate([y[:, q4:2 * q4], y[:, 0:q4], y[:, 3 * q4:], y[:, 2 * q4:3 * q4]], axis=1)
            y = y * cos_ref[...][None] + sw * sin_ref[...][None]
        return y.reshape(N_HEADS, V_DIM, tm)

    q = norm_rope(pT[0:aw], gq_ref[...]) * q_scale
    qT_ref[0] = q.astype(BF16)
    k = norm_rope(pT[aw:2 * aw], gk_ref[...])
    for hh in range(N_HEADS):
        k_ref[0, hh] = k[hh].T.astype(BF16)
    vT_ref[0, :, 0, 0:V_DIM] = pT[2 * aw:3 * aw].reshape(N_HEADS, V_DIM, tm).astype(BF16)
    vT_ref[0, :, 0, V_DIM:] = jnp.ones((N_HEADS, V_ROWS - V_DIM, tm), BF16)


def _inproj_call(x, sh, sc, g, wT, wu, gq, gk, cosT, sinT, *, rope):
    b, n, d = x.shape
    tm = min(TOKEN_TILE, n)
    aw = N_HEADS * V_DIM
    pw = wu.shape[1]
    body = functools.partial(_inproj_body, rope=rope, q_scale=HEAD_DIM ** -0.5 * LOG2_E)
    vec = lambda bb, i: (bb, 0, 0)
    return pl.pallas_call(
        body,
        out_shape=(jax.ShapeDtypeStruct((b, N_HEADS, V_DIM, n), BF16),
                   jax.ShapeDtypeStruct((b, N_HEADS, n, V_DIM), BF16),
                   jax.ShapeDtypeStruct((b, N_HEADS, n // tm, V_ROWS, tm), BF16),
                   jax.ShapeDtypeStruct((b, n, pw), F32)),
        grid=(b, n // tm),
        in_specs=[pl.BlockSpec((1, tm, d), lambda bb, i: (bb, i, 0)),
                  pl.BlockSpec((1, 1, d), vec), pl.BlockSpec((1, 1, d), vec),
                  _const_spec((1, d)), _const_spec((3 * aw, d)), _const_spec((d, pw)),
                  _const_spec((HEAD_DIM, 1)), _const_spec((HEAD_DIM, 1)),
                  pl.BlockSpec((HEAD_DIM, tm), lambda bb, i: (0, i)),
                  pl.BlockSpec((HEAD_DIM, tm), lambda bb, i: (0, i))],
        out_specs=(pl.BlockSpec((1, N_HEADS, V_DIM, tm), lambda bb, i: (bb, 0, 0, i)),
                   pl.BlockSpec((1, N_HEADS, tm, V_DIM), lambda bb, i: (bb, 0, i, 0)),
                   pl.BlockSpec((1, N_HEADS, 1, V_ROWS, tm), lambda bb, i: (bb, 0, i, 0, 0)),
                   pl.BlockSpec((1, tm, pw), lambda bb, i: (bb, i, 0))),
        compiler_params=_params(("arbitrary", "arbitrary"), 56),
        name="inproj_rope" if rope else "inproj_ctx",
    )(x, sh, sc, g, wT, wu, gq, gk, cosT, sinT)


def _attn_body(*refs, n_key_tiles, key_tile, has_ctx, lam_init):
    if has_ctx:
        lam_ref, qT_ref, k_ref, vT_ref, kc_ref, vcT_ref, gs_ref, o_ref, qbd_ref, *bufs = refs
    else:
        lam_ref, qT_ref, k_ref, vT_ref, gs_ref, o_ref, qbd_ref, *bufs = refs
    s_bufs, e_bufs, acc_ref = bufs[0:2], bufs[2:4], bufs[4]
    tq = qT_ref.shape[-1]
    q = qT_ref[0, 0]
    z = jnp.zeros((HEAD_DIM, tq), BF16)
    qbd_ref[0:HEAD_DIM, 0:tq] = q[0:HEAD_DIM]
    qbd_ref[0:HEAD_DIM, tq:] = z
    qbd_ref[HEAD_DIM:, 0:tq] = z
    qbd_ref[HEAD_DIM:, tq:] = q[HEAD_DIM:]

    def scores(kt, slot):
        s = jnp.dot(kt, qbd_ref[...], preferred_element_type=F32)
        s_bufs[slot][0:kt.shape[0]] = s
        return jnp.max(s, axis=0, keepdims=True)

    def softmax(rows, slot, mx, m):
        m_new = jnp.maximum(m, mx)
        alpha = jnp.exp2(m - m_new)
        e_bufs[slot][0:rows] = jnp.exp2((s_bufs[slot][0:rows] - m_new).astype(BF16))
        return alpha, m_new

    def pv(rows, slot, alpha, vt):
        acc_ref[...] = alpha * acc_ref[...] + jnp.dot(vt, e_bufs[slot][0:rows], preferred_element_type=F32)

    def k_tile(j):
        return k_ref[0, 0, pl.ds(pl.multiple_of(j * key_tile, key_tile), key_tile), :]

    tk = key_tile
    m = jnp.full((1, 2 * tq), -jnp.inf, F32)
    acc_ref[...] = jnp.zeros(acc_ref.shape, F32)
    if n_key_tiles == 1:
        mx_c = scores(k_tile(0), 1)
    else:
        mx_p = scores(k_tile(0), 0)
        mx_c = scores(k_tile(1), 1)
        a_p, m = softmax(tk, 0, mx_p, m)

        unroll = max(u for u in (2, 4, 6) if (n_key_tiles - 2) % u == 0)

        def body(t, carry):
            mx_c, a_p, m = carry
            for u in range(unroll):
                i = 2 + unroll * t + u
                mx_n = scores(k_tile(i), u % 2)
                a_c, m = softmax(tk, 1 - u % 2, mx_c, m)
                pv(tk, u % 2, a_p, vT_ref[0, 0, i - 2])
                mx_c, a_p = mx_n, a_c
            return mx_c, a_p, m

        mx_c, a_p, m = lax.fori_loop(0, (n_key_tiles - 2) // unroll, body, (mx_c, a_p, m))
    if has_ctx:
        cl = kc_ref.shape[2]
        mx_x = scores(kc_ref[0, 0], 0)
    a_c, m = softmax(tk, 1, mx_c, m)
    if n_key_tiles > 1:
        pv(tk, 0, a_p, vT_ref[0, 0, n_key_tiles - 2])
    if has_ctx:
        a_x, m = softmax(cl, 0, mx_x, m)
    pv(tk, 1, a_c, vT_ref[0, 0, n_key_tiles - 1])
    if has_ctx:
        pv(cl, 0, a_x, vcT_ref[0, 0, 0])
    acc = acc_ref[0:V_DIM]
    l = acc_ref[V_DIM:V_DIM + 1]

    lv = lam_ref[...]
    lam = (jnp.exp(jnp.sum(lv[0:1] * lv[1:2], axis=1, keepdims=True))
           - jnp.exp(jnp.sum(lv[2:3] * lv[3:4], axis=1, keepdims=True)) + lam_init)
    oT =acc[:, 0:tq] / l[:, 0:tq] - lam * (acc[:, tq:] / l[:, tq:])
    ms = jnp.mean(oT * oT, axis=0, keepdims=True)
    on = (oT * lax.rsqrt(ms + EPS)) * gs_ref[...] * (1.0 - lam_init)
    o_ref[0] = on.T.astype(BF16)


def _attn_call(lam_qk, qT, k, vT, kc, vcT, gs, *, lam_init):
    b, nh, _, n = qT.shape
    nk = k.shape[2]
    n_key_tiles, key_tile = vT.shape[2], vT.shape[4]
    tq = min(Q_TILE, n)
    has_ctx = kc is not None
    assert n_key_tiles == 1 or n_key_tiles % 2 == 0
    assert not has_ctx or kc.shape[2] <= key_tile
    body = functools.partial(_attn_body, n_key_tiles=n_key_tiles, key_tile=key_tile,
                             has_ctx=has_ctx, lam_init=lam_init)
    in_specs = [_const_spec((4, HEAD_DIM)),
                pl.BlockSpec((1, 1, V_DIM, tq), lambda bb, hh, i: (bb, hh, 0, i)),
                pl.BlockSpec((1, 1, nk, V_DIM), lambda bb, hh, i: (bb, hh, 0, 0)),
                pl.BlockSpec((1, 1, n_key_tiles, V_ROWS, key_tile), lambda bb, hh, i: (bb, hh, 0, 0, 0))]
    args = [lam_qk, qT, k, vT]
    if has_ctx:
        cl = kc.shape[2]
        in_specs += [pl.BlockSpec((1, 1, cl, V_DIM), lambda bb, hh, i: (bb, hh, 0, 0)),
                     pl.BlockSpec((1, 1, 1, V_ROWS, cl), lambda bb, hh, i: (bb, hh, 0, 0, 0))]
        args += [kc, vcT]
    in_specs.append(_const_spec((V_DIM, 1)))
    args.append(gs)
    return pl.pallas_call(
        body,
        out_shape=jax.ShapeDtypeStruct((b, n, nh * V_DIM), BF16),
        grid=(b, nh, n // tq),
        in_specs=in_specs,
        out_specs=pl.BlockSpec((1, tq, V_DIM), lambda bb, hh, i: (bb, i, hh)),
        scratch_shapes=[pltpu.VMEM((V_DIM, 2 * tq), BF16),
                        pltpu.VMEM((key_tile, 2 * tq), F32), pltpu.VMEM((key_tile, 2 * tq), F32),
                        pltpu.VMEM((key_tile, 2 * tq), BF16), pltpu.VMEM((key_tile, 2 * tq), BF16),
                        pltpu.VMEM((V_ROWS, 2 * tq), F32)],
        compiler_params=_params(("arbitrary", "arbitrary", "arbitrary"), 48),
        name="attn_latent" if has_ctx else "attn_ctx",
    )(*args)


def _outproj_body(o_ref, u_ref, up_ref, un_ref, x_ref, g1_ref, wp_ref, ps_ref, wo_ref,
                  xo_ref, ubuf, cat, *, n):
    tm = u_ref.shape[1]
    aw = o_ref.shape[2]
    pg = wp_ref.shape[1]
    i = pl.program_id(1)
    last = pl.num_programs(1) - 1
    ubuf[0:HALO] = jnp.where(i > 0, up_ref[0], 0.0)
    ubuf[HALO:HALO + tm] = u_ref[0]
    ubuf[HALO + tm:] = jnp.where(i < last, un_ref[0], 0.0)
    t = i * tm + lax.broadcasted_iota(jnp.int32, (tm, 1), 0)
    cat[:, 0:aw] = o_ref[0]
    for g, w in enumerate(POOL_WINDOWS):
        half = w // 2
        c0 = g * pg
        acc = ubuf[HALO - half:HALO - half + tm, c0:c0 + pg]
        for s in range(-half + 1, w - half):
            acc = acc + ubuf[HALO + s:HALO + s + tm, c0:c0 + pg]
        lo = jnp.maximum(t - half, 0)
        hi = jnp.minimum(t - half + w - 1, n - 1)
        cnt = (hi - lo + 1).astype(F32)
        dlt = acc / cnt - ubuf[HALO:HALO + tm, c0:c0 + pg]
        y = jnp.dot(dlt.astype(BF16), wp_ref[g], preferred_element_type=F32) * ps_ref[:, c0:c0 + pg]
        cat[:, aw + c0:aw + c0 + pg] = y.astype(BF16)
    mix = jnp.dot(cat[...], wo_ref[...], preferred_element_type=F32)
    xo_ref[0] = x_ref[0] + g1_ref[0] * mix


def _outproj_call(o, u, x, g1, wp, ps, wo):
    b, n, d = x.shape
    tm = min(TOKEN_TILE, n)
    aw, pw = o.shape[2], u.shape[2]
    hb = tm // HALO
    nhalo = n // HALO
    body = functools.partial(_outproj_body, n=n)
    return pl.pallas_call(
        body,
        out_shape=jax.ShapeDtypeStruct((b, n, d), F32),
        grid=(b, n // tm),
        in_specs=[pl.BlockSpec((1, tm, aw), lambda bb, i: (bb, i, 0)),
                  pl.BlockSpec((1, tm, pw), lambda bb, i: (bb, i, 0)),
                  pl.BlockSpec((1, HALO, pw), lambda bb, i: (bb, jnp.maximum(i * hb - 1, 0), 0)),
                  pl.BlockSpec((1, HALO, pw), lambda bb, i: (bb, jnp.minimum((i + 1) * hb, nhalo - 1), 0)),
                  pl.BlockSpec((1, tm, d), lambda bb, i: (bb, i, 0)),
                  pl.BlockSpec((1, 1, d), lambda bb, i: (bb, 0, 0)),
                  _const_spec(wp.shape), _const_spec((1, pw)), _const_spec(wo.shape)],
        out_specs=pl.BlockSpec((1, tm, d), lambda bb, i: (bb, i, 0)),
        scratch_shapes=[pltpu.VMEM((tm + 2 * HALO, pw), F32),
                        pltpu.VMEM((tm, aw + pw), BF16)],
        compiler_params=_params(("arbitrary", "arbitrary"), 48),
        name="outproj",
    )(o, u, u, u, x, g1, wp, ps, wo)


def _first_argmax(vals):
    best = vals[0]
    idx = jnp.zeros(best.shape, jnp.int32)
    for j in range(1, len(vals)):
        c = vals[j] > best
        best = jnp.where(c, vals[j], best)
        idx = jnp.where(c, j, idx)
    return best, idx


def _ln2_body(x_ref, sh_ref, sc_ref, g_ref, wr_ref, br_ref, *rest):
    h_ref, e_ref, gcol_ref = rest[-3:]
    tm = x_ref.shape[0]
    h = _rms_modulate(x_ref[...], g_ref[...], sc_ref[0], sh_ref[0])
    h_ref[...] = h
    hh = h.astype(BF16)
    hl = (h - hh.astype(F32)).astype(BF16)
    wr = wr_ref[...]
    wh = wr.astype(BF16)
    wl = (wr - wh.astype(F32)).astype(BF16)
    dg = lambda a, bb: lax.dot_general(a, bb, NT_DIMS, preferred_element_type=F32)
    lg = dg(wh, hh) + dg(wh, hl) + dg(wl, hh) + br_ref[...]
    mx = jnp.max(lg, axis=0, keepdims=True)
    ex = jnp.exp(lg - mx)
    p = ex / jnp.sum(ex, axis=0, keepdims=True)
    rows = [p[e:e + 1] for e in range(N_EXPERTS)]
    scores = []
    for g in range(N_EXPERT_GROUPS):
        r = rows[g * EXPERTS_PER_GROUP:(g + 1) * EXPERTS_PER_GROUP]
        best = None
        for a in range(EXPERTS_PER_GROUP):
            for bb in range(a + 1, EXPERTS_PER_GROUP):
                s = r[a] + r[bb]
                best = s if best is None else jnp.maximum(best, s)
        scores.append(best)
    _, gsel = _first_argmax(scores)
    vals = []
    for j in range(EXPERTS_PER_GROUP):
        v = rows[j]
        for g in range(1, N_EXPERT_GROUPS):
            v = jnp.where(gsel == g, rows[g * EXPERTS_PER_GROUP + j], v)
        vals.append(v)
    v1, i1 = _first_argmax(vals)
    v2 = jnp.full(v1.shape, -jnp.inf, F32)
    i2 = jnp.zeros(i1.shape, jnp.int32)
    for j in range(EXPERTS_PER_GROUP):
        c = jnp.logical_and(i1 != j, vals[j] > v2)
        v2 = jnp.where(c, vals[j], v2)
        i2 = jnp.where(c, j, i2)
    den = v1 + v2
    e1 = gsel * EXPERTS_PER_GROUP + i1
    e2 = gsel * EXPERTS_PER_GROUP + i2
    row2 = lax.broadcasted_iota(jnp.int32, (2, tm), 0)
    e_ref[...] = jnp.where(row2 == 0, e1, e2)
    rowg = lax.broadcasted_iota(jnp.int32, (LANES, tm), 0)
    gates = jnp.where(rowg == 0, v1 / den, jnp.where(rowg == 1, v2 / den, 0.0))
    gcol_ref[...] = gates.T


def _ln2_call(x, sh, sc, g, wr, br, *, total, offset=0, bufs=None):
    b, n, d = x.shape
    tm = min(TOKEN_TILE, n)
    per = n // tm
    t = b * n
    off = offset // tm
    nt = t // tm
    steps = nt if bufs is not None else total // tm
    assert total % tm == 0 and offset % tm == 0
    src = lambda i: jnp.minimum(i, nt - 1)
    in_specs = [pl.BlockSpec((tm, d), lambda i: (src(i), 0)),
                pl.BlockSpec((1, 1, d), lambda i: (src(i) // per, 0, 0)),
                pl.BlockSpec((1, 1, d), lambda i: (src(i) // per, 0, 0)),
                _const_spec((1, d)), _const_spec(wr.shape), _const_spec(br.shape)]
    args = [x.reshape(t, d), sh, sc, g, wr, br]
    aliases = {}
    if bufs is not None:
        in_specs += [pl.BlockSpec(memory_space=pl.ANY)] * 3
        aliases = {len(args) + j: j for j in range(3)}
        args += list(bufs)
    return pl.pallas_call(
        _ln2_body,
        out_shape=(jax.ShapeDtypeStruct((total, d), F32),
                   jax.ShapeDtypeStruct((2, total), jnp.int32),
                   jax.ShapeDtypeStruct((total, LANES), F32)),
        grid=(steps,),
        in_specs=in_specs,
        out_specs=(pl.BlockSpec((tm, d), lambda i: (i + off, 0)),
                   pl.BlockSpec((2, tm), lambda i: (0, i + off)),
                   pl.BlockSpec((tm, LANES), lambda i: (i + off, 0))),
        input_output_aliases=aliases,
        compiler_params=_params(("arbitrary",), 40),
        name="ln2_router",
    )(*args)


def _dispatch_body(e_ref, dest_ref, cnt_ref, cnt_sc, carry_sc, start_sc, *, rows):
    ph = pl.program_id(0)
    j = pl.program_id(1)
    tt = e_ref.shape[1]
    e = e_ref[...]
    eid = lax.broadcasted_iota(jnp.int32, (N_EXPERTS, tt), 0)
    oh0 = (eid == e[0:1]).astype(F32)
    oh1 = (eid == e[1:2]).astype(F32)
    tot0 = jnp.sum(oh0, axis=1, keepdims=True)
    tot1 = jnp.sum(oh1, axis=1, keepdims=True)

    @pl.when(jnp.logical_and(ph == 0, j == 0))
    def _():
        cnt_sc[...] = jnp.zeros(cnt_sc.shape, F32)

    @pl.when(ph == 0)
    def _():
        cnt_sc[...] += tot0 + tot1

    @pl.when(jnp.logical_and(ph == 1, j == 0))
    def _():
        cnt = cnt_sc[...]
        padded = jnp.ceil(cnt / rows) * rows
        ecol = lax.broadcasted_iota(jnp.int32, (N_EXPERTS, 1), 0)
        start = jnp.zeros((N_EXPERTS, 1), F32)
        for ee in range(N_EXPERTS - 1):
            start = start + jnp.where(ecol > ee, padded[ee:ee + 1], 0.0)
        start_sc[...] = start
        carry_sc[...] = jnp.zeros(carry_sc.shape, F32)
        cnt_ref[...] = cnt

    @pl.when(ph == 1)
    def _():
        r = lax.broadcasted_iota(jnp.int32, (tt, tt), 0)
        c = lax.broadcasted_iota(jnp.int32, (tt, tt), 1)
        upper = (r < c).astype(BF16)
        pre0 = jnp.dot(oh0.astype(BF16), upper, preferred_element_type=F32)
        pre1 = jnp.dot(oh1.astype(BF16), upper, preferred_element_type=F32)
        base = start_sc[...] + carry_sc[...]
        d0 = jnp.sum(oh0 * (base + pre0), axis=0, keepdims=True)
        d1 = jnp.sum(oh1 * (base + tot0 + pre1), axis=0, keepdims=True)
        row2 = lax.broadcasted_iota(jnp.int32, (2, tt), 0)
        dest_ref[...] = jnp.where(row2 == 0, d0, d1).astype(jnp.int32)
        carry_sc[...] += tot0 + tot1


def _dispatch_call(experts, *, rows):
    t = experts.shape[1]
    tt = TOKEN_TILE
    body = functools.partial(_dispatch_body, rows=rows)
    return pl.pallas_call(
        body,
        out_shape=(jax.ShapeDtypeStruct((2, t), jnp.int32),
                   jax.ShapeDtypeStruct((N_EXPERTS, 1), F32)),
        grid=(2, t // tt),
        in_specs=[pl.BlockSpec((2, tt), lambda ph, j: (0, j))],
        out_specs=(pl.BlockSpec((2, tt), lambda ph, j: (0, j * ph)),
                   pl.BlockSpec((N_EXPERTS, 1), lambda ph, j: (0, 0))),
        scratch_shapes=[pltpu.VMEM((N_EXPERTS, 1), F32)] * 3,
        compiler_params=_params(("arbitrary", "arbitrary"), 32),
        name="dispatch",
    )(experts)


def _row_copy(src, src_row, dst, dst_row, sem):
    return pltpu.make_async_copy(src.at[pl.ds(src_row, 1)], dst.at[pl.ds(dst_row, 1)], sem)


def _scatter_body(dest_ref, pend_ref, h_ref, xs_ref, zbuf, sem, zsem, *, n_tokens, rows):
    tm = h_ref.shape[0]
    base = pl.program_id(0) * tm

    n_rows = xs_ref.shape[0]
    used = pend_ref[N_EXPERTS - 1]

    def zero_copy(start):
        return pltpu.make_async_copy(zbuf, xs_ref.at[pl.ds(pl.multiple_of(start, rows), rows)], zsem)

    blocks = [(pend_ref[e] > (pend_ref[e - 1] if e else 0), pend_ref[e] - rows) for e in range(N_EXPERTS)]
    blocks += [(used + j * rows < n_rows, used + j * rows) for j in range(N_EXPERTS)]

    @pl.when(pl.program_id(0) == 0)
    def _():
        zbuf[...] = jnp.zeros(zbuf.shape, zbuf.dtype)
        for cond, start in blocks:
            pl.when(cond)(lambda start=start: zero_copy(start).start())
        for cond, start in blocks:
            pl.when(cond)(lambda start=start: zero_copy(start).wait())

    def issue(r, carry):
        for kk in range(2):
            _row_copy(h_ref, r, xs_ref, dest_ref[kk * n_tokens + base + r], sem).start(priority=kk)
        return carry

    lax.fori_loop(0, tm, issue, 0, unroll=DMA_ISSUE_UNROLL)
    for kk in range(2):
        pltpu.make_async_copy(h_ref, xs_ref.at[pl.ds(0, tm)], sem).wait()


def _scatter_call(dest_flat, pad_ends, tokens, *, n_rows, rows):
    t, d = tokens.shape
    tm = TOKEN_TILE
    body = functools.partial(_scatter_body, n_tokens=t, rows=rows)
    return pl.pallas_call(
        body,
        out_shape=jax.ShapeDtypeStruct((n_rows, d), tokens.dtype),
        grid_spec=pltpu.PrefetchScalarGridSpec(
            num_scalar_prefetch=2,
            grid=(t // tm,),
            in_specs=[pl.BlockSpec((tm, d), lambda i, dst, pe: (i, 0))],
            out_specs=pl.BlockSpec(memory_space=pl.ANY),
            scratch_shapes=[pltpu.VMEM((rows, d), tokens.dtype),
                            pltpu.SemaphoreType.DMA(()), pltpu.SemaphoreType.DMA(())]),
        compiler_params=_params(("arbitrary",), 32),
        name="moe_scatter",
    )(dest_flat, pad_ends, tokens)


def _ffn_body(be_ref, nu_ref, xs_ref, wgT_ref, wuT_ref, wdT_ref, ys_ref):
    i = pl.program_id(0)

    @pl.when(i < nu_ref[0])
    def _():
        xb = xs_ref[...].astype(BF16)
        aT = lax.dot_general(wgT_ref[0, 0], xb, NT_DIMS, preferred_element_type=F32)
        bT = lax.dot_general(wuT_ref[0, 0], xb, NT_DIMS, preferred_element_type=F32)
        hT = (_silu(aT) * bT).astype(BF16)
        yT = jnp.dot(wdT_ref[0, 0], hT, preferred_element_type=F32)
        ys_ref[...] = yT.T

    @pl.when(i >= nu_ref[0])
    def _():
        ys_ref[...] = jnp.zeros(ys_ref.shape, F32)


def _ffn_call(block_e, n_used, xs, wgT, wuT, wdT, *, rows, layer):
    p, d = xs.shape
    de = wgT.shape[2]
    return pl.pallas_call(
        _ffn_body,
        out_shape=jax.ShapeDtypeStruct((p, d), F32),
        grid_spec=pltpu.PrefetchScalarGridSpec(
            num_scalar_prefetch=2,
            grid=(p // rows,),
            in_specs=[pl.BlockSpec((rows, d), lambda i, be, nu: (jnp.minimum(i, nu[0] - 1), 0)),
                      pl.BlockSpec((1, 1, de, d), lambda i, be, nu: (layer, be[i], 0, 0)),
                      pl.BlockSpec((1, 1, de, d), lambda i, be, nu: (layer, be[i], 0, 0)),
                      pl.BlockSpec((1, 1, d, de), lambda i, be, nu: (layer, be[i], 0, 0))],
            out_specs=pl.BlockSpec((rows, d), lambda i, be, nu: (i, 0))),
        compiler_params=_params(("arbitrary",), 56),
        name="moe_ffn",
    )(block_e, n_used, xs, wgT, wuT, wdT)


def _combine_body(dest_ref, x_ref, gcol_ref, g2_ref, ys_ref, xo_ref, ya, yb, sem, *, n_tokens, offset):
    tm = x_ref.shape[0]
    base = offset + pl.program_id(0) * tm

    def issue(r, carry):
        _row_copy(ys_ref, dest_ref[base + r], ya, r, sem).start(priority=0)
        _row_copy(ys_ref, dest_ref[n_tokens + base + r], yb, r, sem).start(priority=1)
        return carry

    lax.fori_loop(0, tm, issue, 0, unroll=DMA_ISSUE_UNROLL)
    pltpu.make_async_copy(ys_ref.at[pl.ds(0, tm)], ya, sem).wait()
    pltpu.make_async_copy(ys_ref.at[pl.ds(0, tm)], yb, sem).wait()
    gc = gcol_ref[...]
    y = gc[:, 0:1] * ya[...] + gc[:, 1:2] * yb[...]
    xo_ref[...] = x_ref[...] + g2_ref[0] * y


def _combine_call(dest_flat, x, gcol, g2, ys, *, n_tokens, offset):
    b, n, d = x.shape
    tm = min(TOKEN_TILE, n)
    per = n // tm
    t = b * n
    off_blocks = offset // tm
    body = functools.partial(_combine_body, n_tokens=n_tokens, offset=offset)
    out = pl.pallas_call(
        body,
        out_shape=jax.ShapeDtypeStruct((t, d), F32),
        grid_spec=pltpu.PrefetchScalarGridSpec(
            num_scalar_prefetch=1,
            grid=(t // tm,),
            in_specs=[pl.BlockSpec((tm, d), lambda i, dst: (i, 0)),
                      pl.BlockSpec((tm, LANES), lambda i, dst: (i + off_blocks, 0)),
                      pl.BlockSpec((1, 1, d), lambda i, dst: (i // per, 0, 0)),
                      pl.BlockSpec(memory_space=pl.ANY)],
            out_specs=pl.BlockSpec((tm, d), lambda i, dst: (i, 0)),
            scratch_shapes=[pltpu.VMEM((tm, d), F32), pltpu.VMEM((tm, d), F32),
                            pltpu.SemaphoreType.DMA(())]),
        compiler_params=_params(("arbitrary",), 40),
        name="moe_combine",
    )(dest_flat, x.reshape(t, d), gcol, g2, ys)
    return out.reshape(b, n, d)


def _rope_tables(n):
    rows = n // GRID_W
    row = jnp.repeat(jnp.arange(rows), GRID_W).astype(F32)
    col = jnp.tile(jnp.arange(GRID_W), rows).astype(F32)
    nf = HEAD_DIM // 4
    inv = ROPE_THETA ** (-jnp.arange(nf, dtype=F32) / nf)
    ar = (row[:, None] * inv).T
    ac = (col[:, None] * inv).T
    cosT = jnp.concatenate([jnp.cos(ar), jnp.cos(ar), jnp.cos(ac), jnp.cos(ac)], axis=0)
    sinT = jnp.concatenate([-jnp.sin(ar), jnp.sin(ar), -jnp.sin(ac), jnp.sin(ac)], axis=0)
    return cosT, sinT


def _moe(tokens, experts, w_gateT, w_upT, w_downT, layer):
    t, d = tokens.shape
    rows = MOE_ROWS
    n_blocks = -(-2 * t // rows) + N_EXPERTS
    dest, counts = _dispatch_call(experts, rows=rows)
    cnt = counts[:, 0].astype(jnp.int32)
    blocks = (cnt + rows - 1) // rows
    ends = jnp.cumsum(blocks)
    n_used = ends[-1]
    bid = jnp.arange(n_blocks, dtype=jnp.int32)
    block_e = jnp.sum(jnp.minimum(bid, n_used - 1)[:, None] >= ends[None, :], axis=1).astype(jnp.int32)
    block_e = jnp.minimum(block_e, N_EXPERTS - 1)
    dest_flat = dest.reshape(2 * t)
    xs = _scatter_call(dest_flat, (ends * rows).astype(jnp.int32), tokens, n_rows=n_blocks * rows, rows=rows)
    ys = _ffn_call(block_e, n_used.reshape(1).astype(jnp.int32), xs, w_gateT, w_upT, w_downT,
                   rows=rows, layer=layer)
    return ys, dest_flat


def kernel(x, c, ctx, c_ctx, w_mod, b_mod, norm1, norm2, w_in, q_norm, k_norm, lam_qk, sub_norm, w_pool,
           pool_scale, w_out, w_router, b_router, w_gate, w_up, w_down):
    b, n, d = x.shape
    cl = ctx.shape[1]
    depth = w_mod.shape[0]
    aw = N_HEADS * V_DIM
    assert n % TOKEN_TILE == 0 and n % GRID_W == 0 and cl % SUBLANES == 0
    assert (b * n) % TOKEN_TILE == 0 and (b * cl) % TOKEN_TILE == 0

    c8 = jnp.concatenate([c, c_ctx[None], jnp.zeros((SUBLANES - b - 1, d), F32)], axis=0)
    mod = _mod_call(c8, w_mod, b_mod)
    cosT, sinT = _rope_tables(n)
    zc = jnp.zeros((HEAD_DIM, cl), F32)
    wrT = w_router.T
    br = b_router.reshape(N_EXPERTS, 1)
    wgT = jnp.swapaxes(w_gate, 2, 3).astype(BF16)
    wuT = jnp.swapaxes(w_up, 2, 3).astype(BF16)
    wdT = jnp.swapaxes(w_down, 2, 3).astype(BF16)

    xc = ctx
    for li in range(depth):
        last = li == depth - 1
        lam_init = 0.8 - 0.6 * math.exp(-0.3 * li)
        m6 = mod[li].reshape(SUBLANES, 6, d)
        lat = [m6[0:b, j].reshape(b, 1, d) for j in range(6)]
        cxm = [jnp.broadcast_to(m6[b, j].reshape(1, 1, d), (b, 1, d)) for j in range(6)]

        wi = w_in[li]
        wT = wi[:, :3 * aw].T.astype(BF16)
        wu = wi[:, 3 * aw:].astype(BF16)
        g1n = norm1[li].reshape(1, d)
        g2n = norm2[li].reshape(1, d)
        gq = q_norm[li].reshape(HEAD_DIM, 1)
        gk = k_norm[li].reshape(HEAD_DIM, 1)
        gs = sub_norm[li].reshape(V_DIM, 1)
        wp = w_pool[li].astype(BF16)
        ps = pool_scale[li].reshape(1, -1)
        wo = w_out[li].astype(BF16)

        qT, k, vT, u = _inproj_call(x, lat[0], lat[1], g1n, wT, wu, gq, gk, cosT, sinT, rope=True)
        qcT, kc, vcT, uc = _inproj_call(xc, cxm[0], cxm[1], g1n, wT, wu, gq, gk, zc, zc, rope=False)
        o = _attn_call(lam_qk[li], qT, k, vT, kc, vcT, gs, lam_init=lam_init)
        x = _outproj_call(o, u, x, lat[2], wp, ps, wo)
        if not last:
            oc = _attn_call(lam_qk[li], qcT, kc, vcT, None, None, gs, lam_init=lam_init)
            xc = _outproj_call(oc, uc, xc, cxm[2], wp, ps, wo)

        n_tok = b * n if last else b * (n + cl)
        tok, ex, gcol = _ln2_call(x, lat[3], lat[4], g2n, wrT, br, total=n_tok)
        if not last:
            tok, ex, gcol = _ln2_call(xc, cxm[3], cxm[4], g2n, wrT, br, total=n_tok, offset=b * n,
                                      bufs=(tok, ex, gcol))
        ys, dest_flat = _moe(tok, ex, wgT, wuT, wdT, li)
        x = _combine_call(dest_flat, x, gcol, lat[5], ys, n_tokens=n_tok, offset=0)
        if not last:
            xc = _combine_call(dest_flat, xc, gcol, cxm[5], ys, n_tokens=n_tok, offset=b * n)
    return x
```
